```python
import jax, jax.numpy as jnp
from jax import lax

D_MODEL = 1024
BATCH = 8
SEQ = 4096
DEPTH = 1

FOX_HEADS = 8
FOX_HEAD_DIM = 64
FOX_WIDTH = FOX_HEADS * FOX_HEAD_DIM
HGRN_HEADS = 4
HGRN_EXPAND = 128
HGRN_WIDTH = D_MODEL - FOX_WIDTH
HGRN_HEAD_V = HGRN_WIDTH // HGRN_HEADS
HGRN_KEY_WIDTH = HGRN_HEADS * HGRN_EXPAND
MIX_WIDTH = FOX_WIDTH + HGRN_WIDTH
IN_SPLITS = (FOX_WIDTH, FOX_WIDTH, FOX_WIDTH, FOX_HEADS,
             HGRN_KEY_WIDTH, HGRN_KEY_WIDTH, HGRN_WIDTH, HGRN_WIDTH)
IN_WIDTH = sum(IN_SPLITS)
Q_BLOCK = 128
HGRN_CHUNK = 64
PEER_HEADS = 8
PEER_NKEYS = 128
PEER_NEXPERTS = PEER_NKEYS * PEER_NKEYS
PEER_QDIM = 256
PEER_HALF = PEER_QDIM // 2
PEER_TOPK = 16
PEER_TOKEN_BLOCK = 128
FOX_FGATE_BIAS_INIT = 2.0
LN_EPS = 1e-5
RMS_EPS = 1e-6
DEEPNORM_ALPHA = (2 * DEPTH) ** 0.25
DEEPNORM_BETA = (8 * DEPTH) ** -0.25

kernel_name = "hybrid_fox_hgrn2_peer_deepnorm"

F32 = jnp.float32


def _layer_norm(x, g, b):
    xf = x.astype(F32)
    mu = jnp.mean(xf, axis=-1, keepdims=True)
    var = jnp.mean(jnp.square(xf - mu), axis=-1, keepdims=True)
    y = (xf - mu) * lax.rsqrt(var + LN_EPS) * g.astype(F32) + b.astype(F32)
    return y.astype(x.dtype)


def _head_rms_norm(x, g, out_dtype):
    xf = x.astype(F32)
    y = xf * lax.rsqrt(jnp.mean(jnp.square(xf), axis=-1, keepdims=True) + RMS_EPS) * g.astype(F32)
    return y.astype(out_dtype)


def _split_cols(proj):
    outs, start = [], 0
    for w in IN_SPLITS:
        outs.append(proj[..., start:start + w])
        start += w
    return outs


def _fox_attention(q, k, v, log_f):
    B, S, H, Dh = q.shape
    nb = S // Q_BLOCK
    c = jnp.cumsum(log_f, axis=1)
    cT = c.transpose(0, 2, 1)
    kf = k.astype(F32)
    scale = Dh ** -0.5
    qb = q.astype(F32).reshape(B, nb, Q_BLOCK, H, Dh).transpose(1, 0, 2, 3, 4)
    cb = c.reshape(B, nb, Q_BLOCK, H).transpose(1, 0, 2, 3)
    key_pos = jnp.arange(S)

    def block(args):
        i, q_i, c_i = args
        logits = jnp.einsum('bqhd,bkhd->bhqk', q_i, kf) * scale
        logits = logits + c_i.transpose(0, 2, 1)[..., None] - cT[:, :, None, :]
        q_pos = i * Q_BLOCK + jnp.arange(Q_BLOCK)
        mask = key_pos[None, :] <= q_pos[:, None]
        logits = jnp.where(mask, logits, -jnp.inf)
        p = jax.nn.softmax(logits, axis=-1)
        return jnp.einsum('bhqk,bkhd->bqhd', p.astype(v.dtype), v)

    out = lax.map(block, (jnp.arange(nb), qb, cb))
    return out.transpose(1, 0, 2, 3, 4).reshape(B, S, H, Dh)


def _hgrn2_recurrence(q, k, v, log_f):
    B, S, H, DK = q.shape
    DV = v.shape[-1]
    C = HGRN_CHUNK
    n = S // C

    def to_chunks(t):
        return t.reshape(B, n, C, H, t.shape[-1]).transpose(1, 0, 3, 2, 4)

    qc, kc, vc, gc = to_chunks(q), to_chunks(k), to_chunks(v), to_chunks(log_f)
    causal = jnp.tril(jnp.ones((C, C), dtype=bool))

    def step(state, inp):
        q_i, k_i, v_i, g_i = inp
        b = jnp.cumsum(g_i, axis=2)
        o_inter = jnp.einsum('bhtk,bhkv->bhtv', q_i * jnp.exp(b), state)
        diff = b[:, :, :, None, :] - b[:, :, None, :, :]
        decay = jnp.exp(jnp.where(causal[:, :, None], diff, -jnp.inf))
        scores = jnp.einsum('bhtk,bhtsk,bhsk->bhts', q_i, decay, k_i)
        o = o_inter + jnp.einsum('bhts,bhsv->bhtv', scores, v_i)
        b_last = b[:, :, -1:, :]
        k_dec = k_i * jnp.exp(b_last - b)
        new_state = jnp.exp(b_last[:, :, 0, :])[..., None] * state + jnp.einsum('bhsk,bhsv->bhkv', k_dec, v_i)
        return new_state, o

    s0 = jnp.zeros((B, H, DK, DV), F32)
    _, o = lax.scan(step, s0, (qc, kc, vc, gc))
    return o.transpose(1, 0, 3, 2, 4).reshape(B, S, H, DV)


def _hybrid_mixer(x, w_in, fox_fgate_b, hgrn_fgate_b, lb, fox_out_g, hgrn_out_g, w_out):
    B, S, _ = x.shape
    proj = x @ w_in
    fq, fk, fv, ff, hq, hf, hi, hg = _split_cols(proj)
    fq = fq.reshape(B, S, FOX_HEADS, FOX_HEAD_DIM)
    fk = fk.reshape(B, S, FOX_HEADS, FOX_HEAD_DIM)
    fv = fv.reshape(B, S, FOX_HEADS, FOX_HEAD_DIM)
    fox_log_f = jax.nn.log_sigmoid((ff + fox_fgate_b).astype(F32))
    fox_o = _fox_attention(fq, fk, fv, fox_log_f)
    fox_o = _head_rms_norm(fox_o, fox_out_g.reshape(FOX_HEADS, FOX_HEAD_DIM), x.dtype)
    lb = lb.reshape(HGRN_HEADS, HGRN_EXPAND)
    z = (hf + hgrn_fgate_b).astype(F32).reshape(B, S, HGRN_HEADS, HGRN_EXPAND)
    h_log_f = jnp.logaddexp(jnp.log(lb), jnp.log1p(-lb) + jax.nn.log_sigmoid(z))
    h_k = -jnp.expm1(h_log_f)
    h_q = hq.astype(F32).reshape(B, S, HGRN_HEADS, HGRN_EXPAND)
    h_v = hi.astype(F32).reshape(B, S, HGRN_HEADS, HGRN_HEAD_V)
    h_o = _hgrn2_recurrence(h_q, h_k, h_v, h_log_f)
    h_gate = jax.nn.silu(hg.astype(F32).reshape(B, S, HGRN_HEADS, HGRN_HEAD_V))
    h_o = (_head_rms_norm(h_o, hgrn_out_g.reshape(HGRN_HEADS, HGRN_HEAD_V), F32) * h_gate).astype(x.dtype)
    mixed = jnp.concatenate([fox_o.reshape(B, S, FOX_WIDTH), h_o.reshape(B, S, HGRN_WIDTH)], axis=-1)
    return mixed @ w_out


def _peer_ffn(x, w_q, sub_keys, u_tab, v_tab):
    B, S, D = x.shape
    T = B * S
    xt = x.reshape(T, D)
    q = (xt @ w_q).astype(F32).reshape(T, PEER_HEADS, 2, PEER_HALF)
    s = jnp.einsum('thpd,hpnd->thpn', q, sub_keys.astype(F32))
    s_top, i_top = lax.top_k(s, PEER_TOPK)
    cand = s_top[:, :, 0, :, None] + s_top[:, :, 1, None, :]
    cand_idx = i_top[:, :, 0, :, None] * PEER_NKEYS + i_top[:, :, 1, None, :]
    cand = cand.reshape(T, PEER_HEADS, PEER_TOPK * PEER_TOPK)
    cand_idx = cand_idx.reshape(T, PEER_HEADS, PEER_TOPK * PEER_TOPK)
    best, pos = lax.top_k(cand, PEER_TOPK)
    idx = jnp.take_along_axis(cand_idx, pos, axis=-1)
    gates = jax.nn.softmax(best, axis=-1)
    E = PEER_HEADS * PEER_TOPK
    nblk = T // PEER_TOKEN_BLOCK

    def blk(args):
        x_b, idx_b, g_b = args
        u = u_tab[idx_b]
        h = jax.nn.gelu(jnp.einsum('td,ted->te', x_b, u), approximate=False) * g_b
        return jnp.einsum('te,ted->td', h, v_tab[idx_b])

    y = lax.map(blk, (xt.reshape(nblk, PEER_TOKEN_BLOCK, D),
                      idx.reshape(nblk, PEER_TOKEN_BLOCK, E),
                      gates.astype(x.dtype).reshape(nblk, PEER_TOKEN_BLOCK, E)))
    return y.reshape(B, S, D)


def setup_inputs(seed: int = 0) -> dict:
    key = jax.random.key(seed)
    ks = jax.random.split(key, 20)
    L = DEPTH
    nrm = jax.random.normal
    col_scale = jnp.concatenate([
        jnp.ones((2 * FOX_WIDTH,), F32),
        jnp.full((FOX_WIDTH,), DEEPNORM_BETA, F32),
        jnp.ones((FOX_HEADS + 2 * HGRN_KEY_WIDTH,), F32),
        jnp.full((HGRN_WIDTH,), DEEPNORM_BETA, F32),
        jnp.ones((HGRN_WIDTH,), F32)])
    return {
        "x": nrm(ks[0], (BATCH, SEQ, D_MODEL), F32),
        "ln_in_g": 1.0 + 0.02 * nrm(ks[1], (D_MODEL,), F32),
        "ln_in_b": 0.02 * nrm(ks[2], (D_MODEL,), F32),
        "w_in": nrm(ks[3], (L, D_MODEL, IN_WIDTH), F32) * (D_MODEL ** -0.5) * col_scale,
        "fox_fgate_b": FOX_FGATE_BIAS_INIT + 0.1 * nrm(ks[4], (L, FOX_HEADS), F32),
        "hgrn_fgate_b": 0.02 * nrm(ks[5], (L, HGRN_KEY_WIDTH), F32),
        "hgrn_lb_logits": 0.1 * nrm(ks[6], (DEPTH + 1, HGRN_KEY_WIDTH), F32),
        "fox_out_g": 1.0 + 0.02 * nrm(ks[7], (L, FOX_WIDTH), F32),
        "hgrn_out_g": 1.0 + 0.02 * nrm(ks[8], (L, HGRN_WIDTH), F32),
        "w_out": nrm(ks[9], (L, MIX_WIDTH, D_MODEL), F32) * (MIX_WIDTH ** -0.5) * DEEPNORM_BETA,
        "ln_mix_g": 1.0 + 0.02 * nrm(ks[10], (L, D_MODEL), F32),
        "ln_mix_b": 0.02 * nrm(ks[11], (L, D_MODEL), F32),
        "peer_w_q": nrm(ks[12], (L, D_MODEL, PEER_HEADS * PEER_QDIM), F32) * (D_MODEL ** -0.5),
        "peer_sub_keys": nrm(ks[13], (L, PEER_HEADS, 2, PEER_NKEYS, PEER_HALF), F32) * (PEER_HALF ** -0.5),
        "peer_u": nrm(ks[14], (L, PEER_NEXPERTS, D_MODEL), F32) * (D_MODEL ** -0.5) * DEEPNORM_BETA,
        "peer_v": nrm(ks[15], (L, PEER_NEXPERTS, D_MODEL), F32) * DEEPNORM_BETA,
        "ln_ffn_g": 1.0 + 0.02 * nrm(ks[16], (L, D_MODEL), F32),
        "ln_ffn_b": 0.02 * nrm(ks[17], (L, D_MODEL), F32),
    }


def reference(x, ln_in_g, ln_in_b, w_in, fox_fgate_b, hgrn_fgate_b, hgrn_lb_logits,
              fox_out_g, hgrn_out_g, w_out, ln_mix_g, ln_mix_b, peer_w_q, peer_sub_keys,
              peer_u, peer_v, ln_ffn_g, ln_ffn_b):
    x = _layer_norm(x, ln_in_g, ln_in_b)
    lb_all = jnp.cumsum(jax.nn.softmax(hgrn_lb_logits.astype(F32), axis=0), axis=0)
    for l in range(DEPTH):
        mix = _hybrid_mixer(x, w_in[l], fox_fgate_b[l], hgrn_fgate_b[l], lb_all[l],
                            fox_out_g[l], hgrn_out_g[l], w_out[l])
        x = _layer_norm(DEEPNORM_ALPHA * x + mix, ln_mix_g[l], ln_mix_b[l])
        ffn = _peer_ffn(x, peer_w_q[l], peer_sub_keys[l], peer_u[l], peer_v[l])
        x = _layer_norm(DEEPNORM_ALPHA * x + ffn, ln_ffn_g[l], ln_ffn_b[l])
    return x
```

```python
import functools
import math

import jax
import jax.numpy as jnp
from jax import lax
from jax.experimental import pallas as pl
from jax.experimental.pallas import tpu as pltpu

F32 = jnp.float32
BF16 = jnp.bfloat16

D_MODEL = 1024
FOX_HEADS = 8
FOX_HEAD_DIM = 64
FOX_WIDTH = FOX_HEADS * FOX_HEAD_DIM
HGRN_HEADS = 4
HGRN_DIM = 128
HGRN_WIDTH = HGRN_HEADS * HGRN_DIM
PEER_HEADS = 8
PEER_NKEYS = 128
PEER_TOPK = 16
PEER_HALF = 128
PEER_EXPERTS_PER_TOKEN = PEER_HEADS * PEER_TOPK
LN_EPS = 1e-5
RMS_EPS = 1e-6
DEPTH = 1
DEEPNORM_ALPHA = (2 * DEPTH) ** 0.25
LOG2E = math.log2(math.e)

LANES = 128
GATE_ROWS = 16
HGRN_CHUNK = 128
HGRN_SUB = 16
VMEM_LIMIT = 48 * 1024 * 1024

_NT = (((1,), (1,)), ((), ()))


def _layer_norm(x, g, b):
    mu = jnp.mean(x, axis=-1, keepdims=True)
    xc = x - mu
    var = jnp.mean(xc * xc, axis=-1, keepdims=True)
    return xc * lax.rsqrt(var + LN_EPS) * g + b


def _log_sigmoid_parts(z):
    e = jnp.exp(-jnp.abs(z))
    return jnp.minimum(z, 0.0) - jnp.log(1.0 + e), e


def _inproj_body(x_ref, g_ref, b_ref, wm_ref, wft_ref, fb_ref,
                 xn_ref, q_ref, k_ref, v_ref, c_ref, hq_ref, hf_ref, hi_ref, hg_ref,
                 carry_ref, *, tm, steps_per_batch):
    i = pl.program_id(0)
    xn = _layer_norm(x_ref[...], g_ref[...], b_ref[...])
    xn_ref[...] = xn
    xb = xn.astype(BF16)

    def seg(j):
        return jnp.dot(xb, wm_ref[:, j * 512:(j + 1) * 512], preferred_element_type=F32)

    qs = seg(0) * (FOX_HEAD_DIM ** -0.5 * LOG2E)
    low = lax.broadcasted_iota(jnp.int32, (tm, LANES), 1) < FOX_HEAD_DIM
    for p in range(FOX_HEADS // 2):
        s = qs[:, p * LANES:(p + 1) * LANES]
        q_ref[:, (2 * p) * LANES:(2 * p + 1) * LANES] = jnp.where(low, s, 0.0).astype(BF16)
        q_ref[:, (2 * p + 1) * LANES:(2 * p + 2) * LANES] = jnp.where(low, 0.0, s).astype(BF16)
    k_ref[...] = seg(1).astype(BF16)
    v_ref[...] = seg(2).astype(BF16)
    hq_ref[...] = seg(3)
    hf_ref[...] = seg(4)
    hi_ref[...] = seg(5)
    hg_ref[...] = seg(6)

    z = lax.dot_general(wft_ref[...], xb, _NT, preferred_element_type=F32) + fb_ref[...]
    lf, _ = _log_sigmoid_parts(z)
    r = lax.broadcasted_iota(jnp.int32, (tm, tm), 0)
    c = lax.broadcasted_iota(jnp.int32, (tm, tm), 1)
    upper = (r <= c).astype(F32)
    cs = jnp.dot(lf, upper, precision=lax.Precision.HIGHEST, preferred_element_type=F32)

    @pl.when(i % steps_per_batch == 0)
    def _():
        carry_ref[...] = jnp.zeros_like(carry_ref)

    cs = cs + carry_ref[:, 0:1]
    carry_ref[...] = jnp.broadcast_to(cs[:, tm - 1:tm], carry_ref.shape)
    c_ref[...] = cs * LOG2E


def _inproj(x2, ln_g, ln_b, w_main, w_ft, fgate_b, *, seq, tm=256):
    T = x2.shape[0]
    n = T // tm
    row = lambda i: (i, 0)
    const = lambda i: (0, 0)
    f32_512 = jax.ShapeDtypeStruct((T, 512), F32)
    bf_512 = jax.ShapeDtypeStruct((T, 512), BF16)
    blk512 = pl.BlockSpec((tm, 512), row)
    return pl.pallas_call(
        functools.partial(_inproj_body, tm=tm, steps_per_batch=seq // tm),
        grid=(n,),
        in_specs=[
            pl.BlockSpec((tm, D_MODEL), row),
            pl.BlockSpec((1, D_MODEL), const),
            pl.BlockSpec((1, D_MODEL), const),
            pl.BlockSpec(w_main.shape, const),
            pl.BlockSpec(w_ft.shape, const),
            pl.BlockSpec((GATE_ROWS, tm), const),
        ],
        out_specs=[
            pl.BlockSpec((tm, D_MODEL), row),
            pl.BlockSpec((tm, 2 * FOX_WIDTH), row),
            blk512, blk512,
            pl.BlockSpec((GATE_ROWS, tm), lambda i: (0, i)),
            blk512, blk512, blk512, blk512,
        ],
        out_shape=[
            jax.ShapeDtypeStruct((T, D_MODEL), F32),
            jax.ShapeDtypeStruct((T, 2 * FOX_WIDTH), BF16),
            bf_512, bf_512,
            jax.ShapeDtypeStruct((GATE_ROWS, T), F32),
            f32_512, f32_512, f32_512, f32_512,
        ],
        scratch_shapes=[pltpu.VMEM((GATE_ROWS, LANES), F32)],
        compiler_params=pltpu.CompilerParams(
            dimension_semantics=("arbitrary",), vmem_limit_bytes=VMEM_LIMIT),
        name="ln_inproj",
    )(x2, ln_g, ln_b, w_main, w_ft, fgate_b)


def _fox_body(q_ref, k_ref, v_ref, c_ref, g_ref, o_ref, *, tq):
    qi = pl.program_id(2)
    q0 = pl.multiple_of(qi * tq, tq)
    lane = lax.broadcasted_iota(jnp.int32, (tq, LANES), 1)
    row = lax.broadcasted_iota(jnp.int32, (tq, tq), 0)
    col = lax.broadcasted_iota(jnp.int32, (tq, tq), 1)
    causal = col <= row
    out = jnp.zeros((tq, LANES), F32)
    for hh in range(2):
        qh = q_ref[:, hh * LANES:(hh + 1) * LANES]
        c0 = c_ref[hh:hh + 1, pl.ds(q0, LANES)][:, 0:1]

        def block(j0, m, l, acc, masked):
            kj = k_ref[pl.ds(j0, tq), :]
            vj = v_ref[pl.ds(j0, tq), :]
            s = lax.dot_general(qh, kj, _NT, preferred_element_type=F32)
            z = s + (c0 - c_ref[hh:hh + 1, pl.ds(j0, tq)])
            if masked:
                z = jnp.where(causal, z, -jnp.inf)
            m_new = jnp.maximum(m, jnp.max(z, axis=-1, keepdims=True))
            alpha = jnp.exp2(m - m_new)
            p = jnp.exp2(z - m_new)
            l = alpha * l + jnp.sum(p, axis=-1, keepdims=True)
            acc = alpha * acc + jnp.dot(p.astype(BF16), vj, preferred_element_type=F32)
            return m_new, l, acc

        def step(j, carry):
            return block(pl.multiple_of(j * tq, tq), *carry, masked=False)

        init = (jnp.full((tq, 1), -jnp.inf, F32), jnp.zeros((tq, 1), F32),
                jnp.zeros((tq, LANES), F32))
        m, l, acc = lax.fori_loop(0, qi, step, init)
        m, l, acc = block(q0, m, l, acc, masked=True)

        mine = (lane < FOX_HEAD_DIM) if hh == 0 else (lane >= FOX_HEAD_DIM)
        o = jnp.where(mine, acc / l, 0.0)
        ms = jnp.sum(o * o, axis=-1, keepdims=True) * (1.0 / FOX_HEAD_DIM)
        out = out + o * lax.rsqrt(ms + RMS_EPS)
    o_ref[...] = (out * g_ref[...]).astype(BF16)


def _fox_attention(q, k, v, c, gain, *, tq=512):
    B, S, _ = k.shape
    pairs = FOX_HEADS // 2
    return pl.pallas_call(
        functools.partial(_fox_body, tq=tq),
        grid=(B, pairs, S // tq),
        in_specs=[
            pl.BlockSpec((None, tq, 2 * LANES), lambda b, p, i: (b, i, p)),
            pl.BlockSpec((None, S, LANES), lambda b, p, i: (b, 0, p)),
            pl.BlockSpec((None, S, LANES), lambda b, p, i: (b, 0, p)),
            pl.BlockSpec((None, None, 2, S), lambda b, p, i: (b, p, 0, 0)),
            pl.BlockSpec((1, LANES), lambda b, p, i: (0, p)),
        ],
        out_specs=pl.BlockSpec((None, tq, LANES), lambda b, p, i: (b, i, p)),
        out_shape=jax.ShapeDtypeStruct((B, S, FOX_WIDTH), BF16),
        compiler_params=pltpu.CompilerParams(
            dimension_semantics=("arbitrary", "arbitrary", "arbitrary"),
            vmem_limit_bytes=VMEM_LIMIT),
        name="fox_attention",
    )(q, k, v, c, gain)


def _hgrn_body(hq_ref, hf_ref, hi_ref, hg_ref, fb_ref, lb_ref, og_ref, o_ref, state_ref, *, rows):
    C = HGRN_CHUNK

    @pl.when(pl.program_id(1) == 0)
    def _():
        state_ref[...] = jnp.zeros_like(state_ref)

    r = lax.broadcasted_iota(jnp.int32, (C, C), 0)
    c = lax.broadcasted_iota(jnp.int32, (C, C), 1)
    lower = (c <= r).astype(F32)
    srow = lax.broadcasted_iota(jnp.int32, (C, HGRN_DIM), 0)
    tpos = lax.broadcasted_iota(jnp.int32, (HGRN_SUB, C), 0)
    spos = lax.broadcasted_iota(jnp.int32, (HGRN_SUB, C), 1)

    for h in range(HGRN_HEADS):
        cols = slice(h * HGRN_DIM, (h + 1) * HGRN_DIM)
        lb = lb_ref[:, cols]
        log_lb = jnp.log(lb)
        log_1mlb = jnp.log(1.0 - lb)
        for ch in range(rows // C):
            rs = slice(ch * C, (ch + 1) * C)
            z = hf_ref[rs, cols] + fb_ref[:, cols]
            ls, e = _log_sigmoid_parts(z)
            t = log_1mlb + ls
            g = jnp.maximum(log_lb, t) + jnp.log(1.0 + jnp.exp(-jnp.abs(log_lb - t)))
            kk = (1.0 - lb) * jnp.where(z >= 0.0, e, 1.0) / (1.0 + e)
            q = hq_ref[rs, cols]
            vb = hi_ref[rs, cols].astype(BF16)
            b = jnp.dot(lower, g, precision=lax.Precision.HIGHEST, preferred_element_type=F32)
            b_last = b[C - 1:C, :]
            state = state_ref[h]
            o = jnp.dot((q * jnp.exp(b)).astype(BF16), state.astype(BF16),
                        preferred_element_type=F32)
            parts = []
            for blk in range(C // HGRN_SUB):
                lo = blk * HGRN_SUB
                hi = lo + HGRN_SUB
                ref = b[lo - 1:lo, :] if blk > 0 else jnp.zeros((1, HGRN_DIM), F32)
                qs = (q[lo:hi] * jnp.exp(b[lo:hi] - ref)).astype(BF16)
                ks = (kk * jnp.exp(jnp.where(srow < hi, ref - b, -1e30))).astype(BF16)
                sc = lax.dot_general(qs, ks, _NT, preferred_element_type=F32)
                sc = jnp.where(spos <= tpos + lo, sc, 0.0)
                parts.append(jnp.dot(sc.astype(BF16), vb, preferred_element_type=F32))
            o = o + jnp.concatenate(parts, axis=0)
            kdec = kk * jnp.exp(b_last - b)
            kv = jnp.dot(kdec.T.astype(BF16), vb, preferred_element_type=F32)
            decay = jnp.broadcast_to(jnp.exp(b_last), (HGRN_DIM, HGRN_DIM)).T
            state_ref[h] = decay * state + kv
            ms = jnp.mean(o * o, axis=-1, keepdims=True)
            gate_in = hg_ref[rs, cols]
            silu = gate_in / (1.0 + jnp.exp(-gate_in))
            o_ref[rs, cols] = (o * lax.rsqrt(ms + RMS_EPS) * og_ref[:, cols] * silu).astype(BF16)


def _hgrn(hq, hf, hi, hg, fgate_b, lb, out_g, *, rows=256):
    B, S, W = hq.shape
    blk = pl.BlockSpec((None, rows, W), lambda b, i: (b, i, 0))
    vec = pl.BlockSpec((1, W), lambda b, i: (0, 0))
    return pl.pallas_call(
        functools.partial(_hgrn_body, rows=rows),
        grid=(B, S // rows),
        in_specs=[blk, blk, blk, blk, vec, vec, vec],
        out_specs=blk,
        out_shape=jax.ShapeDtypeStruct((B, S, W), BF16),
        scratch_shapes=[pltpu.VMEM((HGRN_HEADS, HGRN_DIM, HGRN_DIM), F32)],
        compiler_params=pltpu.CompilerParams(
            dimension_semantics=("arbitrary", "arbitrary"), vmem_limit_bytes=VMEM_LIMIT),
        name="hgrn2",
    )(hq, hf, hi, hg, fgate_b, lb, out_g)


def _outproj_body(fo_ref, ho_ref, xn_ref, wa_ref, wb_ref, g_ref, b_ref, o_ref):
    mix = jnp.dot(fo_ref[...], wa_ref[...], preferred_element_type=F32)
    mix = mix + jnp.dot(ho_ref[...], wb_ref[...], preferred_element_type=F32)
    o_ref[...] = _layer_norm(DEEPNORM_ALPHA * xn_ref[...] + mix, g_ref[...], b_ref[...])


def _outproj(fox_o, hgrn_o, xn, w_a, w_b, ln_g, ln_b, *, tm=512):
    T = xn.shape[0]
    row = lambda i: (i, 0)
    const = lambda i: (0, 0)
    return pl.pallas_call(
        _outproj_body,
        grid=(T // tm,),
        in_specs=[
            pl.BlockSpec((tm, 512), row), pl.BlockSpec((tm, 512), row),
            pl.BlockSpec((tm, D_MODEL), row),
            pl.BlockSpec((512, D_MODEL), const), pl.BlockSpec((512, D_MODEL), const),
            pl.BlockSpec((1, D_MODEL), const), pl.BlockSpec((1, D_MODEL), const),
        ],
        out_specs=pl.BlockSpec((tm, D_MODEL), row),
        out_shape=jax.ShapeDtypeStruct((T, D_MODEL), F32),
        compiler_params=pltpu.CompilerParams(
            dimension_semantics=("arbitrary",), vmem_limit_bytes=VMEM_LIMIT),
        name="outproj_ln",
    )(fox_o, hgrn_o, xn, w_a, w_b, ln_g, ln_b)


def _top16_rows(scores, payload, n):
    iota = lax.broadcasted_iota(jnp.int32, scores.shape, 0).astype(F32)
    vals, picks = [], []
    cur = scores
    for _ in range(PEER_TOPK):
        m = jnp.max(cur, axis=0, keepdims=True)
        pos = jnp.min(jnp.where(cur == m, iota, float(n)), axis=0, keepdims=True)
        sel = iota == pos
        vals.append(m)
        picks.append(pos if payload is None
                     else jnp.max(jnp.where(sel, payload, -1.0), axis=0, keepdims=True))
        cur = jnp.where(sel, -jnp.inf, cur)
    return jnp.concatenate(vals, axis=0), jnp.concatenate(picks, axis=0)


def _route_body(x_ref, wq_ref, sk_ref, idx_ref, gate_ref):
    xb = x_ref[...].astype(BF16)
    qb = jnp.dot(xb, wq_ref[...], preferred_element_type=F32).astype(BF16)
    tops = []
    for p in range(2):
        s = lax.dot_general(sk_ref[p], qb[:, p * PEER_HALF:(p + 1) * PEER_HALF], _NT,
                            preferred_element_type=F32)
        tops.append(_top16_rows(s, None, PEER_NKEYS))
    (v0, i0), (v1, i1) = tops
    cand = jnp.concatenate([v0[a:a + 1] + v1 for a in range(PEER_TOPK)], axis=0)
    cidx = jnp.concatenate([i0[a:a + 1] * float(PEER_NKEYS) + i1 for a in range(PEER_TOPK)], axis=0)
    best, bidx = _top16_rows(cand, cidx, PEER_TOPK * PEER_TOPK)
    e = jnp.exp(best - best[0:1])
    gate_ref[...] = e / jnp.sum(e, axis=0, keepdims=True)
    idx_ref[...] = bidx.astype(jnp.int32)


def _route(x1, w_q, sub_keys, *, tt=256):
    T = x1.shape[0]
    return pl.pallas_call(
        _route_body,
        grid=(T // tt, PEER_HEADS),
        in_specs=[
            pl.BlockSpec((tt, D_MODEL), lambda i, h: (i, 0)),
            pl.BlockSpec((D_MODEL, 2 * PEER_HALF), lambda i, h: (0, h)),
            pl.BlockSpec((None, 2, PEER_NKEYS, PEER_HALF), lambda i, h: (h, 0, 0, 0)),
        ],
        out_specs=[
            pl.BlockSpec((None, PEER_TOPK, tt), lambda i, h: (h, 0, i)),
            pl.BlockSpec((None, PEER_TOPK, tt), lambda i, h: (h, 0, i)),
        ],
        out_shape=[
            jax.ShapeDtypeStruct((PEER_HEADS, PEER_TOPK, T), jnp.int32),
            jax.ShapeDtypeStruct((PEER_HEADS, PEER_TOPK, T), F32),
        ],
        compiler_params=pltpu.CompilerParams(
            dimension_semantics=("arbitrary", "arbitrary"), vmem_limit_bytes=VMEM_LIMIT),
        name="peer_route",
    )(x1, w_q, sub_keys)


def _peer_body(idx_cur, idx_nxt, x_ref, g2_ref, lng_ref, lnb_ref, tab_ref, o_ref, buf, sem,
               *, tb, nsteps):
    E = PEER_EXPERTS_PER_TOKEN
    i = pl.program_id(0)
    slot = i % 2

    def issue(idx_ref, t, s):
        base = pl.multiple_of(t * E, E)
        for j in range(E):
            pltpu.make_async_copy(tab_ref.at[pl.ds(idx_ref[t, j], 1), :],
                                  buf.at[s, pl.ds(base + j, 1), :], sem.at[s]).start()

    def wait(s):
        pltpu.make_async_copy(buf.at[s], buf.at[s], sem.at[s]).wait()

    @pl.when(i == 0)
    def _():
        def prime(t, carry):
            issue(idx_cur, t, 0)
            return carry
        lax.fori_loop(0, tb, prime, 0)

    wait(slot)
    xb = x_ref[...].astype(BF16)
    rowi = lax.broadcasted_iota(jnp.int32, (tb, 2 * E), 0)
    gates = g2_ref[...]

    def token(t, y):
        issue(idx_nxt, t, 1 - slot)
        words = buf[slot, pl.ds(pl.multiple_of(t * E, E), E), :]
        w = pltpu.bitcast(words, BF16)
        h = lax.dot_general(xb, w, _NT, preferred_element_type=F32)
        gelu = 0.5 * h * (1.0 + lax.erf(h * math.sqrt(0.5)))
        h = jnp.where(rowi == t, gelu * gates, 0.0)
        h = pltpu.roll(h, 1, axis=1).astype(BF16)
        return y + jnp.dot(h, w, preferred_element_type=F32)

    y = lax.fori_loop(0, tb, token, jnp.zeros((tb, D_MODEL), F32))
    o_ref[...] = _layer_norm(DEEPNORM_ALPHA * x_ref[...] + y, lng_ref[...], lnb_ref[...])

    @pl.when(i == nsteps - 1)
    def _():
        wait(1 - slot)


def _peer(idx, gates2, x1, table, ln_g, ln_b, *, tb=16):
    T = x1.shape[0]
    E = PEER_EXPERTS_PER_TOKEN
    n = T // tb
    row = lambda i: (i, 0)
    const = lambda i: (0, 0)
    return pl.pallas_call(
        functools.partial(_peer_body, tb=tb, nsteps=n),
        grid=(n,),
        in_specs=[
            pl.BlockSpec((tb, E), row, memory_space=pltpu.SMEM),
            pl.BlockSpec((tb, E), lambda i: (jnp.minimum(i + 1, n - 1), 0), memory_space=pltpu.SMEM),
            pl.BlockSpec((tb, D_MODEL), row),
            pl.BlockSpec((tb, 2 * E), row),
            pl.BlockSpec((1, D_MODEL), const),
            pl.BlockSpec((1, D_MODEL), const),
            pl.BlockSpec(memory_space=pl.ANY),
        ],
        out_specs=pl.BlockSpec((tb, D_MODEL), row),
        out_shape=jax.ShapeDtypeStruct((T, D_MODEL), F32),
        scratch_shapes=[
            pltpu.VMEM((2, tb * E, D_MODEL), jnp.uint32),
            pltpu.SemaphoreType.DMA((2,)),
        ],
        compiler_params=pltpu.CompilerParams(
            dimension_semantics=("arbitrary",), vmem_limit_bytes=VMEM_LIMIT),
        name="peer_experts",
    )(idx, idx, x1, gates2, ln_g, ln_b, table)


def _bf16_bits(a):
    return lax.bitcast_convert_type(a.astype(BF16), jnp.uint16).astype(jnp.uint32)


def kernel(x, ln_in_g, ln_in_b, w_in, fox_fgate_b, hgrn_fgate_b, hgrn_lb_logits, fox_out_g,
           hgrn_out_g, w_out, ln_mix_g, ln_mix_b, peer_w_q, peer_sub_keys, peer_u, peer_v,
           ln_ffn_g, ln_ffn_b):
    B, S, D = x.shape
    T = B * S
    tm = 256
    row = lambda a: a.reshape(1, -1).astype(F32)

    w = w_in[0]
    o1 = 3 * FOX_WIDTH
    o2 = o1 + FOX_HEADS
    w_main = jnp.concatenate([w[:, :o1], w[:, o2:]], axis=1).astype(BF16)
    w_ft = jnp.zeros((GATE_ROWS, D), F32).at[:FOX_HEADS].set(w[:, o1:o2].T).astype(BF16)
    fb = jnp.zeros((GATE_ROWS,), F32).at[:FOX_HEADS].set(fox_fgate_b[0])
    fb = jnp.broadcast_to(fb[:, None], (GATE_ROWS, tm))

    xn, q, k, v, c, hq, hf, hi, hg = _inproj(
        x.reshape(T, D), row(ln_in_g), row(ln_in_b), w_main, w_ft, fb, seq=S, tm=tm)

    c = c[:FOX_HEADS].reshape(FOX_HEADS // 2, 2, B, S).transpose(2, 0, 1, 3)
    fox_o = _fox_attention(q.reshape(B, S, -1), k.reshape(B, S, -1), v.reshape(B, S, -1), c,
                           row(fox_out_g[0]))

    lb = jnp.cumsum(jax.nn.softmax(hgrn_lb_logits.astype(F32), axis=0), axis=0)[0]
    r3 = lambda a: a.reshape(B, S, -1)
    hgrn_o = _hgrn(r3(hq), r3(hf), r3(hi), r3(hg), row(hgrn_fgate_b[0]), row(lb),
                   row(hgrn_out_g[0]))

    wo = w_out[0].astype(BF16)
    x1 = _outproj(fox_o.reshape(T, -1), hgrn_o.reshape(T, -1), xn, wo[:FOX_WIDTH], wo[FOX_WIDTH:],
                  row(ln_mix_g[0]), row(ln_mix_b[0]))

    idx_t, gate_t = _route(x1, peer_w_q[0].astype(BF16), peer_sub_keys[0].astype(BF16))
    E = PEER_EXPERTS_PER_TOKEN
    idx = idx_t.reshape(E, T).T
    gates = gate_t.reshape(E, T).T
    gates2 = jnp.stack([gates, jnp.zeros_like(gates)], axis=-1).reshape(T, 2 * E)
    table = (_bf16_bits(peer_v[0]) << 16) | _bf16_bits(peer_u[0])

    out = _peer(idx, gates2, x1, table, row(ln_ffn_g[0]), row(ln_ffn_b[0]))
    return out.reshape(B, S, D)
```

```python
import functools
import math

import jax
import jax.numpy as jnp
from jax import lax
from jax.experimental import pallas as pl
from jax.experimental.pallas import tpu as pltpu

F32 = jnp.float32
BF16 = jnp.bfloat16

D_MODEL = 1024
FOX_HEADS = 8
FOX_HEAD_DIM = 64
FOX_WIDTH = FOX_HEADS * FOX_HEAD_DIM
HGRN_HEADS = 4
HGRN_DIM = 128
HGRN_WIDTH = HGRN_HEADS * HGRN_DIM
PEER_HEADS = 8
PEER_NKEYS = 128
PEER_TOPK = 16
PEER_HALF = 128
PEER_EXPERTS_PER_TOKEN = PEER_HEADS * PEER_TOPK
LN_EPS = 1e-5
RMS_EPS = 1e-6
DEPTH = 1
DEEPNORM_ALPHA = (2 * DEPTH) ** 0.25
LOG2E = math.log2(math.e)

LANES = 128
SUBLANES = 8
FEAT_CHUNKS = D_MODEL // LANES
GATE_ROWS = 16
HGRN_CHUNK = 128
HGRN_SUB = 16
VMEM_LIMIT = 48 * 1024 * 1024

_NT = (((1,), (1,)), ((), ()))


def _layer_norm(x, g, b):
    mu = jnp.mean(x, axis=-1, keepdims=True)
    xc = x - mu
    var = jnp.mean(xc * xc, axis=-1, keepdims=True)
    return xc * lax.rsqrt(var + LN_EPS) * g + b


def _log_sigmoid_parts(z):
    e = jnp.exp(-jnp.abs(z))
    return jnp.minimum(z, 0.0) - jnp.log(1.0 + e), e


def _inproj_body(x_ref, g_ref, b_ref, wm_ref, wft_ref, fb_ref,
                 xn_ref, q_ref, k_ref, v_ref, c_ref, hq_ref, hf_ref, hi_ref, hg_ref,
                 carry_ref, *, tm, steps_per_batch):
    i = pl.program_id(0)
    xn = _layer_norm(x_ref[...], g_ref[...], b_ref[...])
    xn_ref[...] = xn
    xb = xn.astype(BF16)

    def seg(j):
        return jnp.dot(xb, wm_ref[:, j * 512:(j + 1) * 512], preferred_element_type=F32)

    qs = seg(0) * (FOX_HEAD_DIM ** -0.5 * LOG2E)
    low = lax.broadcasted_iota(jnp.int32, (tm, LANES), 1) < FOX_HEAD_DIM
    for p in range(FOX_HEADS // 2):
        s = qs[:, p * LANES:(p + 1) * LANES]
        q_ref[:, (2 * p) * LANES:(2 * p + 1) * LANES] = jnp.where(low, s, 0.0).astype(BF16)
        q_ref[:, (2 * p + 1) * LANES:(2 * p + 2) * LANES] = jnp.where(low, 0.0, s).astype(BF16)
    k_ref[...] = seg(1).astype(BF16)
    v_ref[...] = seg(2).astype(BF16)
    hq_ref[...] = seg(3)
    hf_ref[...] = seg(4)
    hi_ref[...] = seg(5)
    hg_ref[...] = seg(6)

    z = lax.dot_general(wft_ref[...], xb, _NT, preferred_element_type=F32) + fb_ref[...]
    lf, _ = _log_sigmoid_parts(z)
    r = lax.broadcasted_iota(jnp.int32, (tm, tm), 0)
    c = lax.broadcasted_iota(jnp.int32, (tm, tm), 1)
    upper = (r <= c).astype(F32)
    cs = jnp.dot(lf, upper, precision=lax.Precision.HIGHEST, preferred_element_type=F32)

    @pl.when(i % steps_per_batch == 0)
    def _():
        carry_ref[...] = jnp.zeros_like(carry_ref)

    cs = cs + carry_ref[:, 0:1]
    carry_ref[...] = jnp.broadcast_to(cs[:, tm - 1:tm], carry_ref.shape)
    c_ref[...] = cs * LOG2E


def _inproj(x2, ln_g, ln_b, w_main, w_ft, fgate_b, *, seq, tm=256):
    T = x2.shape[0]
    n = T // tm
    row = lambda i: (i, 0)
    const = lambda i: (0, 0)
    f32_512 = jax.ShapeDtypeStruct((T, 512), F32)
    bf_512 = jax.ShapeDtypeStruct((T, 512), BF16)
    blk512 = pl.BlockSpec((tm, 512), row)
    return pl.pallas_call(
        functools.partial(_inproj_body, tm=tm, steps_per_batch=seq // tm),
        grid=(n,),
        in_specs=[
            pl.BlockSpec((tm, D_MODEL), row),
            pl.BlockSpec((1, D_MODEL), const),
            pl.BlockSpec((1, D_MODEL), const),
            pl.BlockSpec(w_main.shape, const),
            pl.BlockSpec(w_ft.shape, const),
            pl.BlockSpec((GATE_ROWS, tm), const),
        ],
        out_specs=[
            pl.BlockSpec((tm, D_MODEL), row),
            pl.BlockSpec((tm, 2 * FOX_WIDTH), row),
            blk512, blk512,
            pl.BlockSpec((GATE_ROWS, tm), lambda i: (0, i)),
            blk512, blk512, blk512, blk512,
        ],
        out_shape=[
            jax.ShapeDtypeStruct((T, D_MODEL), F32),
            jax.ShapeDtypeStruct((T, 2 * FOX_WIDTH), BF16),
            bf_512, bf_512,
            jax.ShapeDtypeStruct((GATE_ROWS, T), F32),
            f32_512, f32_512, f32_512, f32_512,
        ],
        scratch_shapes=[pltpu.VMEM((GATE_ROWS, LANES), F32)],
        compiler_params=pltpu.CompilerParams(
            dimension_semantics=("arbitrary",), vmem_limit_bytes=VMEM_LIMIT),
        name="ln_inproj",
    )(x2, ln_g, ln_b, w_main, w_ft, fgate_b)


def _fox_body(q_ref, k_ref, v_ref, c_ref, g_ref, o_ref, *, tq):
    qi = pl.program_id(2)
    q0 = pl.multiple_of(qi * tq, tq)
    lane = lax.broadcasted_iota(jnp.int32, (tq, LANES), 1)
    row = lax.broadcasted_iota(jnp.int32, (tq, tq), 0)
    col = lax.broadcasted_iota(jnp.int32, (tq, tq), 1)
    causal = col <= row
    out = jnp.zeros((tq, LANES), F32)
    for hh in range(2):
        qh = q_ref[:, hh * LANES:(hh + 1) * LANES]
        c0 = c_ref[hh:hh + 1, pl.ds(q0, LANES)][:, 0:1]

        def block(j0, m, l, acc, masked):
            kj = k_ref[pl.ds(j0, tq), :]
            vj = v_ref[pl.ds(j0, tq), :]
            s = lax.dot_general(qh, kj, _NT, preferred_element_type=F32)
            z = s + (c0 - c_ref[hh:hh + 1, pl.ds(j0, tq)])
            if masked:
                z = jnp.where(causal, z, -jnp.inf)
            m_new = jnp.maximum(m, jnp.max(z, axis=-1, keepdims=True))
            alpha = jnp.exp2(m - m_new)
            p = jnp.exp2(z - m_new)
            l = alpha * l + jnp.sum(p, axis=-1, keepdims=True)
            acc = alpha * acc + jnp.dot(p.astype(BF16), vj, preferred_element_type=F32)
            return m_new, l, acc

        def step(j, carry):
            return block(pl.multiple_of(j * tq, tq), *carry, masked=False)

        init = (jnp.full((tq, 1), -jnp.inf, F32), jnp.zeros((tq, 1), F32),
                jnp.zeros((tq, LANES), F32))
        m, l, acc = lax.fori_loop(0, qi, step, init)
        m, l, acc = block(q0, m, l, acc, masked=True)

        mine = (lane < FOX_HEAD_DIM) if hh == 0 else (lane >= FOX_HEAD_DIM)
        o = jnp.where(mine, acc / l, 0.0)
        ms = jnp.sum(o * o, axis=-1, keepdims=True) * (1.0 / FOX_HEAD_DIM)
        out = out + o * lax.rsqrt(ms + RMS_EPS)
    o_ref[...] = (out * g_ref[...]).astype(BF16)


def _fox_attention(q, k, v, c, gain, *, tq=512):
    B, S, _ = k.shape
    pairs = FOX_HEADS // 2
    return pl.pallas_call(
        functools.partial(_fox_body, tq=tq),
        grid=(B, pairs, S // tq),
        in_specs=[
            pl.BlockSpec((None, tq, 2 * LANES), lambda b, p, i: (b, i, p)),
            pl.BlockSpec((None, S, LANES), lambda b, p, i: (b, 0, p)),
            pl.BlockSpec((None, S, LANES), lambda b, p, i: (b, 0, p)),
            pl.BlockSpec((None, None, 2, S), lambda b, p, i: (b, p, 0, 0)),
            pl.BlockSpec((1, LANES), lambda b, p, i: (0, p)),
        ],
        out_specs=pl.BlockSpec((None, tq, LANES), lambda b, p, i: (b, i, p)),
        out_shape=jax.ShapeDtypeStruct((B, S, FOX_WIDTH), BF16),
        compiler_params=pltpu.CompilerParams(
            dimension_semantics=("arbitrary", "arbitrary", "arbitrary"),
            vmem_limit_bytes=VMEM_LIMIT),
        name="fox_attention",
    )(q, k, v, c, gain)


def _hgrn_body(hq_ref, hf_ref, hi_ref, hg_ref, fb_ref, lb_ref, og_ref, o_ref, state_ref, *, rows):
    C = HGRN_CHUNK

    @pl.when(pl.program_id(1) == 0)
    def _():
        state_ref[...] = jnp.zeros_like(state_ref)

    r = lax.broadcasted_iota(jnp.int32, (C, C), 0)
    c = lax.broadcasted_iota(jnp.int32, (C, C), 1)
    lower = (c <= r).astype(F32)
    srow = lax.broadcasted_iota(jnp.int32, (C, HGRN_DIM), 0)
    tpos = lax.broadcasted_iota(jnp.int32, (HGRN_SUB, C), 0)
    spos = lax.broadcasted_iota(jnp.int32, (HGRN_SUB, C), 1)

    for h in range(HGRN_HEADS):
        cols = slice(h * HGRN_DIM, (h + 1) * HGRN_DIM)
        lb = lb_ref[:, cols]
        log_lb = jnp.log(lb)
        log_1mlb = jnp.log(1.0 - lb)
        for ch in range(rows // C):
            rs = slice(ch * C, (ch + 1) * C)
            z = hf_ref[rs, cols] + fb_ref[:, cols]
            ls, e = _log_sigmoid_parts(z)
            t = log_1mlb + ls
            g = jnp.maximum(log_lb, t) + jnp.log(1.0 + jnp.exp(-jnp.abs(log_lb - t)))
            kk = (1.0 - lb) * jnp.where(z >= 0.0, e, 1.0) / (1.0 + e)
            q = hq_ref[rs, cols]
            vb = hi_ref[rs, cols].astype(BF16)
            b = jnp.dot(lower, g, precision=lax.Precision.HIGHEST, preferred_element_type=F32)
            b_last = b[C - 1:C, :]
            state = state_ref[h]
            o = jnp.dot((q * jnp.exp(b)).astype(BF16), state.astype(BF16),
                        preferred_element_type=F32)
            parts = []
            for blk in range(C // HGRN_SUB):
                lo = blk * HGRN_SUB
                hi = lo + HGRN_SUB
                ref = b[lo - 1:lo, :] if blk > 0 else jnp.zeros((1, HGRN_DIM), F32)
                qs = (q[lo:hi] * jnp.exp(b[lo:hi] - ref)).astype(BF16)
                ks = (kk * jnp.exp(jnp.where(srow < hi, ref - b, -1e30))).astype(BF16)
                sc = lax.dot_general(qs, ks, _NT, preferred_element_type=F32)
                sc = jnp.where(spos <= tpos + lo, sc, 0.0)
                parts.append(jnp.dot(sc.astype(BF16), vb, preferred_element_type=F32))
            o = o + jnp.concatenate(parts, axis=0)
            kdec = kk * jnp.exp(b_last - b)
            kv = jnp.dot(kdec.T.astype(BF16), vb, preferred_element_type=F32)
            decay = jnp.broadcast_to(jnp.exp(b_last), (HGRN_DIM, HGRN_DIM)).T
            state_ref[h] = decay * state + kv
            ms = jnp.mean(o * o, axis=-1, keepdims=True)
            gate_in = hg_ref[rs, cols]
            silu = gate_in / (1.0 + jnp.exp(-gate_in))
            o_ref[rs, cols] = (o * lax.rsqrt(ms + RMS_EPS) * og_ref[:, cols] * silu).astype(BF16)


def _hgrn(hq, hf, hi, hg, fgate_b, lb, out_g, *, rows=256):
    B, S, W = hq.shape
    blk = pl.BlockSpec((None, rows, W), lambda b, i: (b, i, 0))
    vec = pl.BlockSpec((1, W), lambda b, i: (0, 0))
    return pl.pallas_call(
        functools.partial(_hgrn_body, rows=rows),
        grid=(B, S // rows),
        in_specs=[blk, blk, blk, blk, vec, vec, vec],
        out_specs=blk,
        out_shape=jax.ShapeDtypeStruct((B, S, W), BF16),
        scratch_shapes=[pltpu.VMEM((HGRN_HEADS, HGRN_DIM, HGRN_DIM), F32)],
        compiler_params=pltpu.CompilerParams(
            dimension_semantics=("arbitrary", "arbitrary"), vmem_limit_bytes=VMEM_LIMIT),
        name="hgrn2",
    )(hq, hf, hi, hg, fgate_b, lb, out_g)


def _outproj_body(fo_ref, ho_ref, xn_ref, wa_ref, wb_ref, g_ref, b_ref, o_ref):
    mix = jnp.dot(fo_ref[...], wa_ref[...], preferred_element_type=F32)
    mix = mix + jnp.dot(ho_ref[...], wb_ref[...], preferred_element_type=F32)
    o_ref[...] = _layer_norm(DEEPNORM_ALPHA * xn_ref[...] + mix, g_ref[...], b_ref[...])


def _outproj(fox_o, hgrn_o, xn, w_a, w_b, ln_g, ln_b, *, tm=512):
    T = xn.shape[0]
    row = lambda i: (i, 0)
    const = lambda i: (0, 0)
    return pl.pallas_call(
        _outproj_body,
        grid=(T // tm,),
        in_specs=[
            pl.BlockSpec((tm, 512), row), pl.BlockSpec((tm, 512), row),
            pl.BlockSpec((tm, D_MODEL), row),
            pl.BlockSpec((512, D_MODEL), const), pl.BlockSpec((512, D_MODEL), const),
            pl.BlockSpec((1, D_MODEL), const), pl.BlockSpec((1, D_MODEL), const),
        ],
        out_specs=pl.BlockSpec((tm, D_MODEL), row),
        out_shape=jax.ShapeDtypeStruct((T, D_MODEL), F32),
        compiler_params=pltpu.CompilerParams(
            dimension_semantics=("arbitrary",), vmem_limit_bytes=VMEM_LIMIT),
        name="outproj_ln",
    )(fox_o, hgrn_o, xn, w_a, w_b, ln_g, ln_b)


def _top16_rows(scores, payload, n):
    iota = lax.broadcasted_iota(jnp.int32, scores.shape, 0).astype(F32)
    vals, picks = [], []
    cur = scores
    for _ in range(PEER_TOPK):
        m = jnp.max(cur, axis=0, keepdims=True)
        pos = jnp.min(jnp.where(cur == m, iota, float(n)), axis=0, keepdims=True)
        sel = iota == pos
        vals.append(m)
        picks.append(pos if payload is None
                     else jnp.max(jnp.where(sel, payload, -1.0), axis=0, keepdims=True))
        cur = jnp.where(sel, -jnp.inf, cur)
    return jnp.concatenate(vals, axis=0), jnp.concatenate(picks, axis=0)


def _route_body(x_ref, wq_ref, sk_ref, idx_ref, gate_ref):
    xb = x_ref[...].astype(BF16)
    qb = jnp.dot(xb, wq_ref[...], preferred_element_type=F32).astype(BF16)
    tops = []
    for p in range(2):
        s = lax.dot_general(sk_ref[p], qb[:, p * PEER_HALF:(p + 1) * PEER_HALF], _NT,
                            preferred_element_type=F32)
        tops.append(_top16_rows(s, None, PEER_NKEYS))
    (v0, i0), (v1, i1) = tops
    cand = jnp.concatenate([v0[a:a + 1] + v1 for a in range(PEER_TOPK)], axis=0)
    cidx = jnp.concatenate([i0[a:a + 1] * float(PEER_NKEYS) + i1 for a in range(PEER_TOPK)], axis=0)
    best, bidx = _top16_rows(cand, cidx, PEER_TOPK * PEER_TOPK)
    e = jnp.exp(best - best[0:1])
    gate_ref[...] = e / jnp.sum(e, axis=0, keepdims=True)
    idx_ref[...] = bidx.astype(jnp.int32)


def _route(x1, w_q, sub_keys, *, tt=256):
    T = x1.shape[0]
    return pl.pallas_call(
        _route_body,
        grid=(T // tt, PEER_HEADS),
        in_specs=[
            pl.BlockSpec((tt, D_MODEL), lambda i, h: (i, 0)),
            pl.BlockSpec((D_MODEL, 2 * PEER_HALF), lambda i, h: (0, h)),
            pl.BlockSpec((None, 2, PEER_NKEYS, PEER_HALF), lambda i, h: (h, 0, 0, 0)),
        ],
        out_specs=[
            pl.BlockSpec((None, PEER_TOPK, tt), lambda i, h: (h, 0, i)),
            pl.BlockSpec((None, PEER_TOPK, tt), lambda i, h: (h, 0, i)),
        ],
        out_shape=[
            jax.ShapeDtypeStruct((PEER_HEADS, PEER_TOPK, T), jnp.int32),
            jax.ShapeDtypeStruct((PEER_HEADS, PEER_TOPK, T), F32),
        ],
        compiler_params=pltpu.CompilerParams(
            dimension_semantics=("arbitrary", "arbitrary"), vmem_limit_bytes=VMEM_LIMIT),
        name="peer_route",
    )(x1, w_q, sub_keys)


def _peer_body(idx_cur, idx_nxt, x_ref, g2_ref, lng_ref, lnb_ref, tab_ref, o_ref,
               buf_a, buf_b, sem, *, tb, nsteps):
    E = PEER_EXPERTS_PER_TOKEN
    i = pl.program_id(0)

    def issue(idx_ref, row, buf, t, s, part):
        for j in range(part * (E // 2), (part + 1) * (E // 2)):
            a, r = divmod(j, SUBLANES)
            pltpu.make_async_copy(tab_ref.at[idx_ref[row, j]],
                                  buf.at[t, a, :, pl.ds(2 * r, 2), :], sem.at[s]).start()

    def wait(buf, s):
        pltpu.make_async_copy(buf, buf, sem.at[s]).wait()

    @pl.when(i == 0)
    def _():
        def prime(t, carry):
            issue(idx_cur, t, buf_a, t, 0, 0)
            issue(idx_cur, t, buf_a, t, 0, 1)
            return carry
        lax.fori_loop(0, tb, prime, 0)

    rowi = lax.broadcasted_iota(jnp.int32, (tb, 2 * E), 0)

    def weights(buf, t):
        return jnp.concatenate(
            [buf[t, :, c].reshape(2 * E, LANES) for c in range(FEAT_CHUNKS)], axis=1)

    def half(buf, r0, prefetch):
        xb = x_ref[r0:r0 + tb, :].astype(BF16)
        h = jnp.zeros((tb, 2 * E), F32)
        for t in range(tb):
            prefetch(t, 0)
            ht = lax.dot_general(xb, weights(buf, t), _NT, preferred_element_type=F32)
            h = jnp.where(rowi == t, ht, h)
        gelu = 0.5 * h * (1.0 + lax.erf(h * math.sqrt(0.5)))
        h = pltpu.roll(gelu * g2_ref[r0:r0 + tb, :], 1, axis=1).astype(BF16)
        y = jnp.zeros((tb, D_MODEL), F32)
        for t in range(tb):
            prefetch(t, 1)
            y = y + jnp.dot(jnp.where(rowi == t, h, jnp.zeros_like(h)), weights(buf, t),
                            preferred_element_type=F32)
        o_ref[r0:r0 + tb, :] = _layer_norm(DEEPNORM_ALPHA * x_ref[r0:r0 + tb, :] + y,
                                           lng_ref[...], lnb_ref[...])

    wait(buf_a, 0)
    half(buf_a, 0, lambda t, part: issue(idx_cur, tb + t, buf_b, t, 1, part))
    wait(buf_b, 1)
    half(buf_b, tb, lambda t, part: issue(idx_nxt, t, buf_a, t, 0, part))

    @pl.when(i == nsteps - 1)
    def _():
        wait(buf_a, 0)


def _peer(idx, gates2, x1, table, ln_g, ln_b, *, tb=16):
    T = x1.shape[0]
    E = PEER_EXPERTS_PER_TOKEN
    n = T // (2 * tb)
    row = lambda i: (i, 0)
    const = lambda i: (0, 0)
    half_buf = pltpu.VMEM((tb, E // SUBLANES, FEAT_CHUNKS, 2 * SUBLANES, LANES), BF16)
    return pl.pallas_call(
        functools.partial(_peer_body, tb=tb, nsteps=n),
        grid=(n,),
        in_specs=[
            pl.BlockSpec((2 * tb, E), row, memory_space=pltpu.SMEM),
            pl.BlockSpec((2 * tb, E), lambda i: (jnp.minimum(i + 1, n - 1), 0),
                         memory_space=pltpu.SMEM),
            pl.BlockSpec((2 * tb, D_MODEL), row),
            pl.BlockSpec((2 * tb, 2 * E), row),
            pl.BlockSpec((1, D_MODEL), const),
            pl.BlockSpec((1, D_MODEL), const),
            pl.BlockSpec(memory_space=pl.ANY),
        ],
        out_specs=pl.BlockSpec((2 * tb, D_MODEL), row),
        out_shape=jax.ShapeDtypeStruct((T, D_MODEL), F32),
        scratch_shapes=[half_buf, half_buf, pltpu.SemaphoreType.DMA((2,))],
        compiler_params=pltpu.CompilerParams(
            dimension_semantics=("arbitrary",), vmem_limit_bytes=VMEM_LIMIT),
        name="peer_experts",
    )(idx, idx, x1, gates2, ln_g, ln_b, table)


def kernel(x, ln_in_g, ln_in_b, w_in, fox_fgate_b, hgrn_fgate_b, hgrn_lb_logits, fox_out_g,
           hgrn_out_g, w_out, ln_mix_g, ln_mix_b, peer_w_q, peer_sub_keys, peer_u, peer_v,
           ln_ffn_g, ln_ffn_b):
    B, S, D = x.shape
    T = B * S
    tm = 256
    row = lambda a: a.reshape(1, -1).astype(F32)

    w = w_in[0]
    o1 = 3 * FOX_WIDTH
    o2 = o1 + FOX_HEADS
    w_main = jnp.concatenate([w[:, :o1], w[:, o2:]], axis=1).astype(BF16)
    w_ft = jnp.zeros((GATE_ROWS, D), F32).at[:FOX_HEADS].set(w[:, o1:o2].T).astype(BF16)
    fb = jnp.zeros((GATE_ROWS,), F32).at[:FOX_HEADS].set(fox_fgate_b[0])
    fb = jnp.broadcast_to(fb[:, None], (GATE_ROWS, tm))

    xn, q, k, v, c, hq, hf, hi, hg = _inproj(
        x.reshape(T, D), row(ln_in_g), row(ln_in_b), w_main, w_ft, fb, seq=S, tm=tm)

    c = c[:FOX_HEADS].reshape(FOX_HEADS // 2, 2, B, S).transpose(2, 0, 1, 3)
    fox_o = _fox_attention(q.reshape(B, S, -1), k.reshape(B, S, -1), v.reshape(B, S, -1), c,
                           row(fox_out_g[0]))

    lb = jnp.cumsum(jax.nn.softmax(hgrn_lb_logits.astype(F32), axis=0), axis=0)[0]
    r3 = lambda a: a.reshape(B, S, -1)
    hgrn_o = _hgrn(r3(hq), r3(hf), r3(hi), r3(hg), row(hgrn_fgate_b[0]), row(lb),
                   row(hgrn_out_g[0]))

    wo = w_out[0].astype(BF16)
    x1 = _outproj(fox_o.reshape(T, -1), hgrn_o.reshape(T, -1), xn, wo[:FOX_WIDTH], wo[FOX_WIDTH:],
                  row(ln_mix_g[0]), row(ln_mix_b[0]))

    idx_t, gate_t = _route(x1, peer_w_q[0].astype(BF16), peer_sub_keys[0].astype(BF16))
    E = PEER_EXPERTS_PER_TOKEN
    idx = idx_t.reshape(E, T).T
    gates = gate_t.reshape(E, T).T
    gates2 = jnp.stack([gates, jnp.zeros_like(gates)], axis=-1).reshape(T, 2 * E)
    n_exp = peer_u.shape[1]
    table = jnp.stack([peer_u[0].reshape(n_exp, FEAT_CHUNKS, LANES),
                       peer_v[0].reshape(n_exp, FEAT_CHUNKS, LANES)], axis=2).astype(BF16)

    out = _peer(idx, gates2, x1, table, row(ln_ffn_g[0]), row(ln_ffn_b[0]))
    return out.reshape(B, S, D)
```

```python
import functools
import math

import jax
import jax.numpy as jnp
from jax import lax
from jax.experimental import pallas as pl
from jax.experimental.pallas import tpu as pltpu

F32 = jnp.float32
BF16 = jnp.bfloat16

D_MODEL = 1024
FOX_HEADS = 8
FOX_HEAD_DIM = 64
FOX_WIDTH = FOX_HEADS * FOX_HEAD_DIM
HGRN_HEADS = 4
HGRN_DIM = 128
HGRN_WIDTH = HGRN_HEADS * HGRN_DIM
PEER_HEADS = 8
PEER_NKEYS = 128
PEER_TOPK = 16
PEER_HALF = 128
PEER_EXPERTS_PER_TOKEN = PEER_HEADS * PEER_TOPK
LN_EPS = 1e-5
RMS_EPS = 1e-6
DEPTH = 1
DEEPNORM_ALPHA = (2 * DEPTH) ** 0.25
LOG2E = math.log2(math.e)

LANES = 128
SUBLANES = 8
FEAT_CHUNKS = D_MODEL // LANES
GATE_ROWS = 16
PEER_RING = 4
PEER_LOOKAHEAD = 2
HGRN_CHUNK = 128
HGRN_SUB = 16
VMEM_LIMIT = 48 * 1024 * 1024

_NT = (((1,), (1,)), ((), ()))


def _layer_norm(x, g, b):
    mu = jnp.mean(x, axis=-1, keepdims=True)
    xc = x - mu
    var = jnp.mean(xc * xc, axis=-1, keepdims=True)
    return xc * lax.rsqrt(var + LN_EPS) * g + b


def _log_sigmoid_parts(z):
    e = jnp.exp(-jnp.abs(z))
    return jnp.minimum(z, 0.0) - jnp.log(1.0 + e), e


def _inproj_body(x_ref, g_ref, b_ref, wm_ref, wft_ref, fb_ref,
                 xn_ref, q_ref, k_ref, v_ref, c_ref, hq_ref, hf_ref, hi_ref, hg_ref,
                 carry_ref, *, tm, steps_per_batch):
    i = pl.program_id(0)
    xn = _layer_norm(x_ref[...], g_ref[...], b_ref[...])
    xn_ref[...] = xn
    xb = xn.astype(BF16)

    def seg(j):
        return jnp.dot(xb, wm_ref[:, j * 512:(j + 1) * 512], preferred_element_type=F32)

    qs = seg(0) * (FOX_HEAD_DIM ** -0.5 * LOG2E)
    low = lax.broadcasted_iota(jnp.int32, (tm, LANES), 1) < FOX_HEAD_DIM
    for p in range(FOX_HEADS // 2):
        s = qs[:, p * LANES:(p + 1) * LANES]
        q_ref[:, (2 * p) * LANES:(2 * p + 1) * LANES] = jnp.where(low, s, 0.0).astype(BF16)
        q_ref[:, (2 * p + 1) * LANES:(2 * p + 2) * LANES] = jnp.where(low, 0.0, s).astype(BF16)
    k_ref[...] = seg(1).astype(BF16)
    v_ref[...] = seg(2).astype(BF16)
    hq_ref[...] = seg(3)
    hf_ref[...] = seg(4)
    hi_ref[...] = seg(5)
    hg_ref[...] = seg(6)

    z = lax.dot_general(wft_ref[...], xb, _NT, preferred_element_type=F32) + fb_ref[...]
    lf, _ = _log_sigmoid_parts(z)
    r = lax.broadcasted_iota(jnp.int32, (tm, tm), 0)
    c = lax.broadcasted_iota(jnp.int32, (tm, tm), 1)
    upper = (r <= c).astype(F32)
    cs = jnp.dot(lf, upper, precision=lax.Precision.HIGHEST, preferred_element_type=F32)

    @pl.when(i % steps_per_batch == 0)
    def _():
        carry_ref[...] = jnp.zeros_like(carry_ref)

    cs = cs + carry_ref[:, 0:1]
    carry_ref[...] = jnp.broadcast_to(cs[:, tm - 1:tm], carry_ref.shape)
    c_ref[...] = cs * LOG2E


def _inproj(x2, ln_g, ln_b, w_main, w_ft, fgate_b, *, seq, tm=256):
    T = x2.shape[0]
    n = T // tm
    row = lambda i: (i, 0)
    const = lambda i: (0, 0)
    f32_512 = jax.ShapeDtypeStruct((T, 512), F32)
    bf_512 = jax.ShapeDtypeStruct((T, 512), BF16)
    blk512 = pl.BlockSpec((tm, 512), row)
    return pl.pallas_call(
        functools.partial(_inproj_body, tm=tm, steps_per_batch=seq // tm),
        grid=(n,),
        in_specs=[
            pl.BlockSpec((tm, D_MODEL), row),
            pl.BlockSpec((1, D_MODEL), const),
            pl.BlockSpec((1, D_MODEL), const),
            pl.BlockSpec(w_main.shape, const),
            pl.BlockSpec(w_ft.shape, const),
            pl.BlockSpec((GATE_ROWS, tm), const),
        ],
        out_specs=[
            pl.BlockSpec((tm, D_MODEL), row),
            pl.BlockSpec((tm, 2 * FOX_WIDTH), row),
            blk512, blk512,
            pl.BlockSpec((GATE_ROWS, tm), lambda i: (0, i)),
            blk512, blk512, blk512, blk512,
        ],
        out_shape=[
            jax.ShapeDtypeStruct((T, D_MODEL), F32),
            jax.ShapeDtypeStruct((T, 2 * FOX_WIDTH), BF16),
            bf_512, bf_512,
            jax.ShapeDtypeStruct((GATE_ROWS, T), F32),
            f32_512, f32_512, f32_512, f32_512,
        ],
        scratch_shapes=[pltpu.VMEM((GATE_ROWS, LANES), F32)],
        compiler_params=pltpu.CompilerParams(
            dimension_semantics=("arbitrary",), vmem_limit_bytes=VMEM_LIMIT),
        name="ln_inproj",
    )(x2, ln_g, ln_b, w_main, w_ft, fgate_b)


def _fox_body(q_ref, k_ref, v_ref, c_ref, g_ref, o_ref, *, tq):
    qi = pl.program_id(2)
    q0 = pl.multiple_of(qi * tq, tq)
    lane = lax.broadcasted_iota(jnp.int32, (tq, LANES), 1)
    row = lax.broadcasted_iota(jnp.int32, (tq, tq), 0)
    col = lax.broadcasted_iota(jnp.int32, (tq, tq), 1)
    causal = col <= row
    qs = [q_ref[:, hh * LANES:(hh + 1) * LANES] for hh in range(2)]
    c0 = [c_ref[hh:hh + 1, pl.ds(q0, LANES)][:, 0:1] for hh in range(2)]

    def block(j0, carry, masked):
        kj = k_ref[pl.ds(j0, tq), :]
        vj = v_ref[pl.ds(j0, tq), :]
        out = []
        for hh in range(2):
            m, l, acc = carry[hh]
            s = lax.dot_general(qs[hh], kj, _NT, preferred_element_type=F32)
            z = s + (c0[hh] - c_ref[hh:hh + 1, pl.ds(j0, tq)])
            if masked:
                z = jnp.where(causal, z, -jnp.inf)
            m_new = jnp.maximum(m, jnp.max(z, axis=-1, keepdims=True))
            alpha = jnp.exp2(m - m_new)
            p = jnp.exp2(z - m_new)
            l = alpha * l + jnp.sum(p, axis=-1, keepdims=True)
            acc = alpha * acc + jnp.dot(p.astype(BF16), vj, preferred_element_type=F32)
            out.append((m_new, l, acc))
        return tuple(out)

    def step(j, carry):
        return block(pl.multiple_of(j * tq, tq), carry, masked=False)

    init = (jnp.full((tq, 1), -jnp.inf, F32), jnp.zeros((tq, 1), F32),
            jnp.zeros((tq, LANES), F32))
    carry = lax.fori_loop(0, qi, step, (init, init))
    carry = block(q0, carry, masked=True)

    out = jnp.zeros((tq, LANES), F32)
    for hh in range(2):
        _, l, acc = carry[hh]
        mine = (lane < FOX_HEAD_DIM) if hh == 0 else (lane >= FOX_HEAD_DIM)
        o = jnp.where(mine, acc / l, 0.0)
        ms = jnp.sum(o * o, axis=-1, keepdims=True) * (1.0 / FOX_HEAD_DIM)
        out = out + o * lax.rsqrt(ms + RMS_EPS)
    o_ref[...] = (out * g_ref[...]).astype(BF16)


def _fox_attention(q, k, v, c, gain, *, tq=512):
    B, S, _ = k.shape
    pairs = FOX_HEADS // 2
    return pl.pallas_call(
        functools.partial(_fox_body, tq=tq),
        grid=(B, pairs, S // tq),
        in_specs=[
            pl.BlockSpec((None, tq, 2 * LANES), lambda b, p, i: (b, i, p)),
            pl.BlockSpec((None, S, LANES), lambda b, p, i: (b, 0, p)),
            pl.BlockSpec((None, S, LANES), lambda b, p, i: (b, 0, p)),
            pl.BlockSpec((None, None, 2, S), lambda b, p, i: (b, p, 0, 0)),
            pl.BlockSpec((1, LANES), lambda b, p, i: (0, p)),
        ],
        out_specs=pl.BlockSpec((None, tq, LANES), lambda b, p, i: (b, i, p)),
        out_shape=jax.ShapeDtypeStruct((B, S, FOX_WIDTH), BF16),
        compiler_params=pltpu.CompilerParams(
            dimension_semantics=("arbitrary", "arbitrary", "arbitrary"),
            vmem_limit_bytes=VMEM_LIMIT),
        name="fox_attention",
    )(q, k, v, c, gain)


def _hgrn_body(hq_ref, hf_ref, hi_ref, hg_ref, fb_ref, lb_ref, og_ref, o_ref, state_ref, *, rows):
    C = HGRN_CHUNK
    n_ch = rows // C
    n_sub = C // HGRN_SUB

    @pl.when(pl.program_id(1) == 0)
    def _():
        state_ref[...] = jnp.zeros_like(state_ref)

    r = lax.broadcasted_iota(jnp.int32, (C, C), 0)
    c = lax.broadcasted_iota(jnp.int32, (C, C), 1)
    lower = (c <= r).astype(F32)
    causal = c <= r
    srow = lax.broadcasted_iota(jnp.int32, (C, HGRN_DIM), 0)

    lb = lb_ref[...]
    log_lb = jnp.log(lb)
    log_1mlb = jnp.log(1.0 - lb)
    z = hf_ref[...] + fb_ref[...]
    ls, e = _log_sigmoid_parts(z)
    t = log_1mlb + ls
    g_all = jnp.maximum(log_lb, t) + jnp.log(1.0 + jnp.exp(-jnp.abs(log_lb - t)))
    k_all = (1.0 - lb) * jnp.where(z >= 0.0, e, 1.0) / (1.0 + e)
    b_all = [jnp.dot(lower, g_all[ch * C:(ch + 1) * C], precision=lax.Precision.HIGHEST,
                     preferred_element_type=F32) for ch in range(n_ch)]

    items = [(h, ch) for ch in range(n_ch) for h in range(HGRN_HEADS)]
    q, kk, vb, b, scores = {}, {}, {}, {}, {}
    for it in items:
        h, ch = it
        cols = slice(h * HGRN_DIM, (h + 1) * HGRN_DIM)
        rs = slice(ch * C, (ch + 1) * C)
        q[it] = hq_ref[rs, cols]
        kk[it] = k_all[rs, cols]
        vb[it] = hi_ref[rs, cols].astype(BF16)
        b[it] = b_all[ch][:, cols]
    for it in items:
        parts = []
        for blk in range(n_sub):
            lo = blk * HGRN_SUB
            hi = lo + HGRN_SUB
            ref = b[it][lo - 1:lo, :] if blk > 0 else jnp.zeros((1, HGRN_DIM), F32)
            qs = (q[it][lo:hi] * jnp.exp(b[it][lo:hi] - ref)).astype(BF16)
            ks = (kk[it] * jnp.exp(jnp.where(srow < hi, ref - b[it], -1e30))).astype(BF16)
            parts.append(lax.dot_general(qs, ks, _NT, preferred_element_type=F32))
        scores[it] = jnp.where(causal, jnp.concatenate(parts, axis=0), 0.0).astype(BF16)
    o_intra = {it: jnp.dot(scores[it], vb[it], preferred_element_type=F32) for it in items}
    kv = {}
    for it in items:
        kdec = kk[it] * jnp.exp(b[it][C - 1:C, :] - b[it])
        kv[it] = jnp.dot(kdec.T.astype(BF16), vb[it], preferred_element_type=F32)
    for h in range(HGRN_HEADS):
        cols = slice(h * HGRN_DIM, (h + 1) * HGRN_DIM)
        state = state_ref[h]
        for ch in range(n_ch):
            it = (h, ch)
            rs = slice(ch * C, (ch + 1) * C)
            o = o_intra[it] + jnp.dot((q[it] * jnp.exp(b[it])).astype(BF16), state.astype(BF16),
                                      preferred_element_type=F32)
            decay = jnp.broadcast_to(jnp.exp(b[it][C - 1:C, :]), (HGRN_DIM, HGRN_DIM)).T
            state = decay * state + kv[it]
            ms = jnp.mean(o * o, axis=-1, keepdims=True)
            gate_in = hg_ref[rs, cols]
            silu = gate_in / (1.0 + jnp.exp(-gate_in))
            o_ref[rs, cols] = (o * lax.rsqrt(ms + RMS_EPS) * og_ref[:, cols] * silu).astype(BF16)
        state_ref[h] = state


def _hgrn(hq, hf, hi, hg, fgate_b, lb, out_g, *, rows=256):
    B, S, W = hq.shape
    blk = pl.BlockSpec((None, rows, W), lambda b, i: (b, i, 0))
    vec = pl.BlockSpec((1, W), lambda b, i: (0, 0))
    return pl.pallas_call(
        functools.partial(_hgrn_body, rows=rows),
        grid=(B, S // rows),
        in_specs=[blk, blk, blk, blk, vec, vec, vec],
        out_specs=blk,
        out_shape=jax.ShapeDtypeStruct((B, S, W), BF16),
        scratch_shapes=[pltpu.VMEM((HGRN_HEADS, HGRN_DIM, HGRN_DIM), F32)],
        compiler_params=pltpu.CompilerParams(
            dimension_semantics=("arbitrary", "arbitrary"), vmem_limit_bytes=VMEM_LIMIT),
        name="hgrn2",
    )(hq, hf, hi, hg, fgate_b, lb, out_g)


def _outproj_body(fo_ref, ho_ref, xn_ref, wa_ref, wb_ref, g_ref, b_ref, o_ref):
    mix = jnp.dot(fo_ref[...], wa_ref[...], preferred_element_type=F32)
    mix = mix + jnp.dot(ho_ref[...], wb_ref[...], preferred_element_type=F32)
    o_ref[...] = _layer_norm(DEEPNORM_ALPHA * xn_ref[...] + mix, g_ref[...], b_ref[...])


def _outproj(fox_o, hgrn_o, xn, w_a, w_b, ln_g, ln_b, *, tm=512):
    T = xn.shape[0]
    row = lambda i: (i, 0)
    const = lambda i: (0, 0)
    return pl.pallas_call(
        _outproj_body,
        grid=(T // tm,),
        in_specs=[
            pl.BlockSpec((tm, 512), row), pl.BlockSpec((tm, 512), row),
            pl.BlockSpec((tm, D_MODEL), row),
            pl.BlockSpec((512, D_MODEL), const), pl.BlockSpec((512, D_MODEL), const),
            pl.BlockSpec((1, D_MODEL), const), pl.BlockSpec((1, D_MODEL), const),
        ],
        out_specs=pl.BlockSpec((tm, D_MODEL), row),
        out_shape=jax.ShapeDtypeStruct((T, D_MODEL), F32),
        compiler_params=pltpu.CompilerParams(
            dimension_semantics=("arbitrary",), vmem_limit_bytes=VMEM_LIMIT),
        name="outproj_ln",
    )(fox_o, hgrn_o, xn, w_a, w_b, ln_g, ln_b)


def _top16_rows(scores, order, payload):
    big = float(PEER_TOPK * PEER_NKEYS)
    vals, picks = [], []
    cur = scores
    for _ in range(PEER_TOPK):
        m = jnp.max(cur, axis=0, keepdims=True)
        pos = jnp.min(jnp.where(cur == m, order, big), axis=0, keepdims=True)
        sel = order == pos
        vals.append(m)
        picks.append(pos if payload is None
                     else jnp.max(jnp.where(sel, payload, -1.0), axis=0, keepdims=True))
        cur = jnp.where(sel, -jnp.inf, cur)
    return jnp.concatenate(vals, axis=0), jnp.concatenate(picks, axis=0)


def _pair_slabs(first, second, combine, fill):
    t = first.shape[1]
    r8 = lax.broadcasted_iota(jnp.int32, (SUBLANES, t), 0)
    slabs = [combine(first[0:1], second)]
    for a in range(1, SUBLANES):
        keep = r8 < PEER_TOPK // (a + 1)
        slabs.append(jnp.where(keep, combine(first[a:a + 1], second[0:SUBLANES]), fill))
    slabs.append(combine(first[SUBLANES:], second[0:1]))
    return jnp.concatenate(slabs, axis=0)


def _route_body(x_ref, wq_ref, sk_ref, idx_ref, gate_ref, *, heads):
    xb = x_ref[...].astype(BF16)
    t = xb.shape[0]
    key_order = lax.broadcasted_iota(jnp.int32, (PEER_NKEYS, t), 0).astype(F32)
    r8 = lax.broadcasted_iota(jnp.int32, (SUBLANES, t), 0)
    flat = [lax.broadcasted_iota(jnp.int32, (PEER_TOPK, t), 0)]
    for a in range(1, SUBLANES):
        flat.append(jnp.where(r8 < PEER_TOPK // (a + 1), r8 + a * PEER_TOPK, PEER_TOPK * PEER_TOPK))
    flat.append((r8 + SUBLANES) * PEER_TOPK)
    flat = jnp.concatenate(flat, axis=0).astype(F32)
    for hh in range(heads):
        w = 2 * PEER_HALF
        qb = jnp.dot(xb, wq_ref[:, hh * w:(hh + 1) * w], preferred_element_type=F32).astype(BF16)
        tops = []
        for p in range(2):
            s = lax.dot_general(sk_ref[hh, p], qb[:, p * PEER_HALF:(p + 1) * PEER_HALF], _NT,
                                preferred_element_type=F32)
            tops.append(_top16_rows(s, key_order, None))
        (v0, i0), (v1, i1) = tops
        cand = _pair_slabs(v0, v1, lambda x, y: x + y, -jnp.inf)
        cidx = _pair_slabs(i0 * float(PEER_NKEYS), i1, lambda x, y: x + y, -1.0)
        best, bidx = _top16_rows(cand, flat, cidx)
        e = jnp.exp(best - best[0:1])
        gate_ref[hh] = e / jnp.sum(e, axis=0, keepdims=True)
        idx_ref[hh] = bidx.astype(jnp.int32)


def _route(x1, w_q, sub_keys, *, tt=256, heads=2):
    T = x1.shape[0]
    return pl.pallas_call(
        functools.partial(_route_body, heads=heads),
        grid=(T // tt, PEER_HEADS // heads),
        in_specs=[
            pl.BlockSpec((tt, D_MODEL), lambda i, h: (i, 0)),
            pl.BlockSpec((D_MODEL, heads * 2 * PEER_HALF), lambda i, h: (0, h)),
            pl.BlockSpec((heads, 2, PEER_NKEYS, PEER_HALF), lambda i, h: (h, 0, 0, 0)),
        ],
        out_specs=[
            pl.BlockSpec((heads, PEER_TOPK, tt), lambda i, h: (h, 0, i)),
            pl.BlockSpec((heads, PEER_TOPK, tt), lambda i, h: (h, 0, i)),
        ],
        out_shape=[
            jax.ShapeDtypeStruct((PEER_HEADS, PEER_TOPK, T), jnp.int32),
            jax.ShapeDtypeStruct((PEER_HEADS, PEER_TOPK, T), F32),
        ],
        compiler_params=pltpu.CompilerParams(
            dimension_semantics=("arbitrary", "arbitrary"), vmem_limit_bytes=VMEM_LIMIT),
        name="peer_route",
    )(x1, w_q, sub_keys)


def _peer_body(idx_cur, idx_nxt, x_ref, g2_ref, lng_ref, lnb_ref, tab_ref, o_ref, *scratch,
               tb, nsteps):
    E = PEER_EXPERTS_PER_TOKEN
    bufs, sem = scratch[:PEER_RING], scratch[PEER_RING]
    i = pl.program_id(0)

    def issue(idx_ref, row, q, t, part):
        for j in range(part * (E // 2), (part + 1) * (E // 2)):
            a, r = divmod(j, SUBLANES)
            pltpu.make_async_copy(tab_ref.at[idx_ref[row, j]],
                                  bufs[q].at[t, a, :, pl.ds(2 * r, 2), :],
                                  sem.at[q]).start(priority=j % 2)

    def wait(q):
        pltpu.make_async_copy(bufs[q], bufs[q], sem.at[q]).wait()

    @pl.when(i == 0)
    def _():
        def prime(t, carry):
            for q in range(PEER_LOOKAHEAD):
                issue(idx_cur, q * tb + t, q, t, 0)
                issue(idx_cur, q * tb + t, q, t, 1)
            return carry
        lax.fori_loop(0, tb, prime, 0)

    rowi = lax.broadcasted_iota(jnp.int32, (tb, 2 * E), 0)

    def weights(q, t):
        return jnp.concatenate(
            [bufs[q][t, :, c].reshape(2 * E, LANES) for c in range(FEAT_CHUNKS)], axis=1)

    for q in range(PEER_RING):
        r0 = q * tb
        ahead = q + PEER_LOOKAHEAD
        if ahead < PEER_RING:
            prefetch = functools.partial(issue, idx_cur, q=ahead)
        else:
            prefetch = functools.partial(issue, idx_nxt, q=ahead - PEER_RING)
        row0 = (ahead % PEER_RING) * tb

        wait(q)
        xb = x_ref[r0:r0 + tb, :].astype(BF16)
        h = jnp.zeros((tb, 2 * E), F32)
        for t in range(tb):
            prefetch(row=row0 + t, t=t, part=0)
            ht = lax.dot_general(xb, weights(q, t), _NT, preferred_element_type=F32)
            h = jnp.where(rowi == t, ht, h)
        gelu = 0.5 * h * (1.0 + lax.erf(h * math.sqrt(0.5)))
        h = pltpu.roll(gelu * g2_ref[r0:r0 + tb, :], 1, axis=1).astype(BF16)
        y = jnp.zeros((tb, D_MODEL), F32)
        for t in range(tb):
            prefetch(row=row0 + t, t=t, part=1)
            y = y + jnp.dot(jnp.where(rowi == t, h, jnp.zeros_like(h)), weights(q, t),
                            preferred_element_type=F32)
        o_ref[r0:r0 + tb, :] = _layer_norm(DEEPNORM_ALPHA * x_ref[r0:r0 + tb, :] + y,
                                           lng_ref[...], lnb_ref[...])

    @pl.when(i == nsteps - 1)
    def _():
        for q in range(PEER_LOOKAHEAD):
            wait(q)


def _peer(idx, gates2, x1, table, ln_g, ln_b, *, tb=16):
    T = x1.shape[0]
    E = PEER_EXPERTS_PER_TOKEN
    step = PEER_RING * tb
    n = T // step
    row = lambda i: (i, 0)
    const = lambda i: (0, 0)
    ring_buf = pltpu.VMEM((tb, E // SUBLANES, FEAT_CHUNKS, 2 * SUBLANES, LANES), BF16)
    return pl.pallas_call(
        functools.partial(_peer_body, tb=tb, nsteps=n),
        grid=(n,),
        in_specs=[
            pl.BlockSpec((step, E), row, memory_space=pltpu.SMEM),
            pl.BlockSpec((step, E), lambda i: (jnp.minimum(i + 1, n - 1), 0),
                         memory_space=pltpu.SMEM),
            pl.BlockSpec((step, D_MODEL), row),
            pl.BlockSpec((step, 2 * E), row),
            pl.BlockSpec((1, D_MODEL), const),
            pl.BlockSpec((1, D_MODEL), const),
            pl.BlockSpec(memory_space=pl.ANY),
        ],
        out_specs=pl.BlockSpec((step, D_MODEL), row),
        out_shape=jax.ShapeDtypeStruct((T, D_MODEL), F32),
        scratch_shapes=[ring_buf] * PEER_RING + [pltpu.SemaphoreType.DMA((PEER_RING,))],
        compiler_params=pltpu.CompilerParams(
            dimension_semantics=("arbitrary",), vmem_limit_bytes=VMEM_LIMIT),
        name="peer_experts",
    )(idx, idx, x1, gates2, ln_g, ln_b, table)


def kernel(x, ln_in_g, ln_in_b, w_in, fox_fgate_b, hgrn_fgate_b, hgrn_lb_logits, fox_out_g,
           hgrn_out_g, w_out, ln_mix_g, ln_mix_b, peer_w_q, peer_sub_keys, peer_u, peer_v,
           ln_ffn_g, ln_ffn_b):
    B, S, D = x.shape
    T = B * S
    tm = 256
    row = lambda a: a.reshape(1, -1).astype(F32)

    w = w_in[0]
    o1 = 3 * FOX_WIDTH
    o2 = o1 + FOX_HEADS
    w_main = jnp.concatenate([w[:, :o1], w[:, o2:]], axis=1).astype(BF16)
    w_ft = jnp.zeros((GATE_ROWS, D), F32).at[:FOX_HEADS].set(w[:, o1:o2].T).astype(BF16)
    fb = jnp.zeros((GATE_ROWS,), F32).at[:FOX_HEADS].set(fox_fgate_b[0])
    fb = jnp.broadcast_to(fb[:, None], (GATE_ROWS, tm))

    xn, q, k, v, c, hq, hf, hi, hg = _inproj(
        x.reshape(T, D), row(ln_in_g), row(ln_in_b), w_main, w_ft, fb, seq=S, tm=tm)

    c = c[:FOX_HEADS].reshape(FOX_HEADS // 2, 2, B, S).transpose(2, 0, 1, 3)
    fox_o = _fox_attention(q.reshape(B, S, -1), k.reshape(B, S, -1), v.reshape(B, S, -1), c,
                           row(fox_out_g[0]))

    lb = jnp.cumsum(jax.nn.softmax(hgrn_lb_logits.astype(F32), axis=0), axis=0)[0]
    r3 = lambda a: a.reshape(B, S, -1)
    hgrn_o = _hgrn(r3(hq), r3(hf), r3(hi), r3(hg), row(hgrn_fgate_b[0]), row(lb),
                   row(hgrn_out_g[0]))

    wo = w_out[0].astype(BF16)
    x1 = _outproj(fox_o.reshape(T, -1), hgrn_o.reshape(T, -1), xn, wo[:FOX_WIDTH], wo[FOX_WIDTH:],
                  row(ln_mix_g[0]), row(ln_mix_b[0]))

    idx_t, gate_t = _route(x1, peer_w_q[0].astype(BF16), peer_sub_keys[0].astype(BF16))
    E = PEER_EXPERTS_PER_TOKEN
    idx = idx_t.reshape(E, T).T
    gates = gate_t.reshape(E, T).T
    gates2 = jnp.stack([gates, jnp.zeros_like(gates)], axis=-1).reshape(T, 2 * E)
    n_exp = peer_u.shape[1]
    table = jnp.stack([peer_u[0].reshape(n_exp, FEAT_CHUNKS, LANES),
                       peer_v[0].reshape(n_exp, FEAT_CHUNKS, LANES)], axis=2).astype(BF16)

    out = _peer(idx, gates2, x1, table, row(ln_ffn_g[0]), row(ln_ffn_b[0]))
    return out.reshape(B, S, D)
```

```python
import functools
import math

import jax
import jax.numpy as jnp
from jax import lax
from jax.experimental import pallas as pl
from jax.experimental.pallas import tpu as pltpu

F32 = jnp.float32
BF16 = jnp.bfloat16

D_MODEL = 1024
FOX_HEADS = 8
FOX_HEAD_DIM = 64
FOX_WIDTH = FOX_HEADS * FOX_HEAD_DIM
HGRN_HEADS = 4
HGRN_DIM = 128
HGRN_WIDTH = HGRN_HEADS * HGRN_DIM
PEER_HEADS = 8
PEER_NKEYS = 128
PEER_TOPK = 16
PEER_HALF = 128
PEER_EXPERTS_PER_TOKEN = PEER_HEADS * PEER_TOPK
LN_EPS = 1e-5
RMS_EPS = 1e-6
DEPTH = 1
DEEPNORM_ALPHA = (2 * DEPTH) ** 0.25
LOG2E = math.log2(math.e)

LANES = 128
SUBLANES = 8
FEAT_CHUNKS = D_MODEL // LANES
GATE_ROWS = 16
PEER_RING = 4
PEER_LOOKAHEAD = 2
HGRN_CHUNK = 128
HGRN_SUB = 16
VMEM_LIMIT = 48 * 1024 * 1024

_NT = (((1,), (1,)), ((), ()))


def _layer_norm(x, g, b):
    mu = jnp.mean(x, axis=-1, keepdims=True)
    xc = x - mu
    var = jnp.mean(xc * xc, axis=-1, keepdims=True)
    return xc * lax.rsqrt(var + LN_EPS) * g + b


def _log_sigmoid_parts(z):
    e = jnp.exp(-jnp.abs(z))
    return jnp.minimum(z, 0.0) - jnp.log(1.0 + e), e


def _inproj_body(x_ref, g_ref, b_ref, wm_ref, wft_ref, fb_ref,
                 xn_ref, q_ref, k_ref, v_ref, c_ref, hq_ref, hf_ref, hi_ref, hg_ref,
                 carry_ref, *, tm, steps_per_batch):
    i = pl.program_id(0)
    xn = _layer_norm(x_ref[...], g_ref[...], b_ref[...])
    xn_ref[...] = xn
    xb = xn.astype(BF16)

    def seg(j):
        return jnp.dot(xb, wm_ref[:, j * 512:(j + 1) * 512], preferred_element_type=F32)

    qs = seg(0) * (FOX_HEAD_DIM ** -0.5 * LOG2E)
    low = lax.broadcasted_iota(jnp.int32, (tm, LANES), 1) < FOX_HEAD_DIM
    for p in range(FOX_HEADS // 2):
        s = qs[:, p * LANES:(p + 1) * LANES]
        q_ref[:, (2 * p) * LANES:(2 * p + 1) * LANES] = jnp.where(low, s, 0.0).astype(BF16)
        q_ref[:, (2 * p + 1) * LANES:(2 * p + 2) * LANES] = jnp.where(low, 0.0, s).astype(BF16)
    k_ref[...] = seg(1).astype(BF16)
    v_ref[...] = seg(2).astype(BF16)
    hq_ref[...] = seg(3)
    hf_ref[...] = seg(4)
    hi_ref[...] = seg(5)
    hg_ref[...] = seg(6)

    z = lax.dot_general(wft_ref[...], xb, _NT, preferred_element_type=F32) + fb_ref[...]
    lf, _ = _log_sigmoid_parts(z)
    r = lax.broadcasted_iota(jnp.int32, (tm, tm), 0)
    c = lax.broadcasted_iota(jnp.int32, (tm, tm), 1)
    upper = (r <= c).astype(F32)
    cs = jnp.dot(lf, upper, precision=lax.Precision.HIGHEST, preferred_element_type=F32)

    @pl.when(i % steps_per_batch == 0)
    def _():
        carry_ref[...] = jnp.zeros_like(carry_ref)

    cs = cs + carry_ref[:, 0:1]
    carry_ref[...] = jnp.broadcast_to(cs[:, tm - 1:tm], carry_ref.shape)
    c_ref[...] = cs * LOG2E


def _inproj(x2, ln_g, ln_b, w_main, w_ft, fgate_b, *, seq, tm=256):
    T = x2.shape[0]
    n = T // tm
    row = lambda i: (i, 0)
    const = lambda i: (0, 0)
    f32_512 = jax.ShapeDtypeStruct((T, 512), F32)
    bf_512 = jax.ShapeDtypeStruct((T, 512), BF16)
    blk512 = pl.BlockSpec((tm, 512), row)
    return pl.pallas_call(
        functools.partial(_inproj_body, tm=tm, steps_per_batch=seq // tm),
        grid=(n,),
        in_specs=[
            pl.BlockSpec((tm, D_MODEL), row),
            pl.BlockSpec((1, D_MODEL), const),
            pl.BlockSpec((1, D_MODEL), const),
            pl.BlockSpec(w_main.shape, const),
            pl.BlockSpec(w_ft.shape, const),
            pl.BlockSpec((GATE_ROWS, tm), const),
        ],
        out_specs=[
            pl.BlockSpec((tm, D_MODEL), row),
            pl.BlockSpec((tm, 2 * FOX_WIDTH), row),
            blk512, blk512,
            pl.BlockSpec((GATE_ROWS, tm), lambda i: (0, i)),
            blk512, blk512, blk512, blk512,
        ],
        out_shape=[
            jax.ShapeDtypeStruct((T, D_MODEL), F32),
            jax.ShapeDtypeStruct((T, 2 * FOX_WIDTH), BF16),
            bf_512, bf_512,
            jax.ShapeDtypeStruct((GATE_ROWS, T), F32),
            f32_512, f32_512, f32_512, f32_512,
        ],
        scratch_shapes=[pltpu.VMEM((GATE_ROWS, LANES), F32)],
        compiler_params=pltpu.CompilerParams(
            dimension_semantics=("arbitrary",), vmem_limit_bytes=VMEM_LIMIT),
        name="ln_inproj",
    )(x2, ln_g, ln_b, w_main, w_ft, fgate_b)


def _fox_body(q_ref, k_ref, v_ref, c_ref, g_ref, o_ref, *, tq):
    qi = pl.program_id(2)
    q0 = pl.multiple_of(qi * tq, tq)
    lane = lax.broadcasted_iota(jnp.int32, (tq, LANES), 1)
    row = lax.broadcasted_iota(jnp.int32, (tq, tq), 0)
    col = lax.broadcasted_iota(jnp.int32, (tq, tq), 1)
    causal = col <= row
    qs = [q_ref[:, hh * LANES:(hh + 1) * LANES] for hh in range(2)]
    c0 = [c_ref[hh:hh + 1, pl.ds(q0, LANES)][:, 0:1] for hh in range(2)]

    def block(j0, carry, masked):
        kj = k_ref[pl.ds(j0, tq), :]
        vj = v_ref[pl.ds(j0, tq), :]
        out = []
        for hh in range(2):
            m, l, acc = carry[hh]
            s = lax.dot_general(qs[hh], kj, _NT, preferred_element_type=F32)
            z = s + (c0[hh] - c_ref[hh:hh + 1, pl.ds(j0, tq)])
            if masked:
                z = jnp.where(causal, z, -jnp.inf)
            m_new = jnp.maximum(m, jnp.max(z, axis=-1, keepdims=True))
            alpha = jnp.exp2(m - m_new)
            p = jnp.exp2(z - m_new)
            l = alpha * l + jnp.sum(p, axis=-1, keepdims=True)
            acc = alpha * acc + jnp.dot(p.astype(BF16), vj, preferred_element_type=F32)
            out.append((m_new, l, acc))
        return tuple(out)

    def step(j, carry):
        return block(pl.multiple_of(j * tq, tq), carry, masked=False)

    init = (jnp.full((tq, 1), -jnp.inf, F32), jnp.zeros((tq, 1), F32),
            jnp.zeros((tq, LANES), F32))
    carry = lax.fori_loop(0, qi, step, (init, init))
    carry = block(q0, carry, masked=True)

    out = jnp.zeros((tq, LANES), F32)
    for hh in range(2):
        _, l, acc = carry[hh]
        mine = (lane < FOX_HEAD_DIM) if hh == 0 else (lane >= FOX_HEAD_DIM)
        o = jnp.where(mine, acc / l, 0.0)
        ms = jnp.sum(o * o, axis=-1, keepdims=True) * (1.0 / FOX_HEAD_DIM)
        out = out + o * lax.rsqrt(ms + RMS_EPS)
    o_ref[...] = (out * g_ref[...]).astype(BF16)


def _fox_attention(q, k, v, c, gain, *, tq=512):
    B, S, _ = k.shape
    pairs = FOX_HEADS // 2
    return pl.pallas_call(
        functools.partial(_fox_body, tq=tq),
        grid=(B, pairs, S // tq),
        in_specs=[
            pl.BlockSpec((None, tq, 2 * LANES), lambda b, p, i: (b, i, p)),
            pl.BlockSpec((None, S, LANES), lambda b, p, i: (b, 0, p)),
            pl.BlockSpec((None, S, LANES), lambda b, p, i: (b, 0, p)),
            pl.BlockSpec((None, None, 2, S), lambda b, p, i: (b, p, 0, 0)),
            pl.BlockSpec((1, LANES), lambda b, p, i: (0, p)),
        ],
        out_specs=pl.BlockSpec((None, tq, LANES), lambda b, p, i: (b, i, p)),
        out_shape=jax.ShapeDtypeStruct((B, S, FOX_WIDTH), BF16),
        compiler_params=pltpu.CompilerParams(
            dimension_semantics=("arbitrary", "arbitrary", "arbitrary"),
            vmem_limit_bytes=VMEM_LIMIT),
        name="fox_attention",
    )(q, k, v, c, gain)


def _hgrn_body(hq_ref, hf_ref, hi_ref, hg_ref, fb_ref, lb_ref, og_ref, o_ref, state_ref, *, rows):
    C = HGRN_CHUNK
    n_ch = rows // C
    n_sub = C // HGRN_SUB

    @pl.when(pl.program_id(1) == 0)
    def _():
        state_ref[...] = jnp.zeros_like(state_ref)

    r = lax.broadcasted_iota(jnp.int32, (C, C), 0)
    c = lax.broadcasted_iota(jnp.int32, (C, C), 1)
    lower = (c <= r).astype(F32)
    causal = c <= r
    srow = lax.broadcasted_iota(jnp.int32, (C, HGRN_DIM), 0)

    lb = lb_ref[...]
    log_lb = jnp.log(lb)
    log_1mlb = jnp.log(1.0 - lb)
    z = hf_ref[...] + fb_ref[...]
    ls, e = _log_sigmoid_parts(z)
    t = log_1mlb + ls
    g_all = jnp.maximum(log_lb, t) + jnp.log(1.0 + jnp.exp(-jnp.abs(log_lb - t)))
    k_all = (1.0 - lb) * jnp.where(z >= 0.0, e, 1.0) / (1.0 + e)
    b_all = [jnp.dot(lower, g_all[ch * C:(ch + 1) * C], precision=lax.Precision.HIGHEST,
                     preferred_element_type=F32) for ch in range(n_ch)]

    items = [(h, ch) for ch in range(n_ch) for h in range(HGRN_HEADS)]
    q, kk, vb, b, scores = {}, {}, {}, {}, {}
    for it in items:
        h, ch = it
        cols = slice(h * HGRN_DIM, (h + 1) * HGRN_DIM)
        rs = slice(ch * C, (ch + 1) * C)
        q[it] = hq_ref[rs, cols]
        kk[it] = k_all[rs, cols]
        vb[it] = hi_ref[rs, cols].astype(BF16)
        b[it] = b_all[ch][:, cols]
    for it in items:
        parts = []
        for blk in range(n_sub):
            lo = blk * HGRN_SUB
            hi = lo + HGRN_SUB
            ref = b[it][lo - 1:lo, :] if blk > 0 else jnp.zeros((1, HGRN_DIM), F32)
            qs = (q[it][lo:hi] * jnp.exp(b[it][lo:hi] - ref)).astype(BF16)
            ks = (kk[it] * jnp.exp(jnp.where(srow < hi, ref - b[it], -1e30))).astype(BF16)
            parts.append(lax.dot_general(qs, ks, _NT, preferred_element_type=F32))
        scores[it] = jnp.where(causal, jnp.concatenate(parts, axis=0), 0.0).astype(BF16)
    o_intra = {it: jnp.dot(scores[it], vb[it], preferred_element_type=F32) for it in items}
    kv = {}
    for it in items:
        kdec = kk[it] * jnp.exp(b[it][C - 1:C, :] - b[it])
        kv[it] = jnp.dot(kdec.T.astype(BF16), vb[it], preferred_element_type=F32)
    for h in range(HGRN_HEADS):
        cols = slice(h * HGRN_DIM, (h + 1) * HGRN_DIM)
        state = state_ref[h]
        for ch in range(n_ch):
            it = (h, ch)
            rs = slice(ch * C, (ch + 1) * C)
            o = o_intra[it] + jnp.dot((q[it] * jnp.exp(b[it])).astype(BF16), state.astype(BF16),
                                      preferred_element_type=F32)
            decay = jnp.broadcast_to(jnp.exp(b[it][C - 1:C, :]), (HGRN_DIM, HGRN_DIM)).T
            state = decay * state + kv[it]
            ms = jnp.mean(o * o, axis=-1, keepdims=True)
            gate_in = hg_ref[rs, cols]
            silu = gate_in / (1.0 + jnp.exp(-gate_in))
            o_ref[rs, cols] = (o * lax.rsqrt(ms + RMS_EPS) * og_ref[:, cols] * silu).astype(BF16)
        state_ref[h] = state


def _hgrn(hq, hf, hi, hg, fgate_b, lb, out_g, *, rows=256):
    B, S, W = hq.shape
    blk = pl.BlockSpec((None, rows, W), lambda b, i: (b, i, 0))
    vec = pl.BlockSpec((1, W), lambda b, i: (0, 0))
    return pl.pallas_call(
        functools.partial(_hgrn_body, rows=rows),
        grid=(B, S // rows),
        in_specs=[blk, blk, blk, blk, vec, vec, vec],
        out_specs=blk,
        out_shape=jax.ShapeDtypeStruct((B, S, W), BF16),
        scratch_shapes=[pltpu.VMEM((HGRN_HEADS, HGRN_DIM, HGRN_DIM), F32)],
        compiler_params=pltpu.CompilerParams(
            dimension_semantics=("arbitrary", "arbitrary"), vmem_limit_bytes=VMEM_LIMIT),
        name="hgrn2",
    )(hq, hf, hi, hg, fgate_b, lb, out_g)


def _outproj_body(fo_ref, ho_ref, xn_ref, wa_ref, wb_ref, g_ref, b_ref, o_ref):
    mix = jnp.dot(fo_ref[...], wa_ref[...], preferred_element_type=F32)
    mix = mix + jnp.dot(ho_ref[...], wb_ref[...], preferred_element_type=F32)
    o_ref[...] = _layer_norm(DEEPNORM_ALPHA * xn_ref[...] + mix, g_ref[...], b_ref[...])


def _outproj(fox_o, hgrn_o, xn, w_a, w_b, ln_g, ln_b, *, tm=512):
    T = xn.shape[0]
    row = lambda i: (i, 0)
    const = lambda i: (0, 0)
    return pl.pallas_call(
        _outproj_body,
        grid=(T // tm,),
        in_specs=[
            pl.BlockSpec((tm, 512), row), pl.BlockSpec((tm, 512), row),
            pl.BlockSpec((tm, D_MODEL), row),
            pl.BlockSpec((512, D_MODEL), const), pl.BlockSpec((512, D_MODEL), const),
            pl.BlockSpec((1, D_MODEL), const), pl.BlockSpec((1, D_MODEL), const),
        ],
        out_specs=pl.BlockSpec((tm, D_MODEL), row),
        out_shape=jax.ShapeDtypeStruct((T, D_MODEL), F32),
        compiler_params=pltpu.CompilerParams(
            dimension_semantics=("arbitrary",), vmem_limit_bytes=VMEM_LIMIT),
        name="outproj_ln",
    )(fox_o, hgrn_o, xn, w_a, w_b, ln_g, ln_b)


def _top16_rows(scores, order, payload):
    big = float(PEER_TOPK * PEER_NKEYS)
    vals, picks = [], []
    cur = scores
    for _ in range(PEER_TOPK):
        m = jnp.max(cur, axis=0, keepdims=True)
        pos = jnp.min(jnp.where(cur == m, order, big), axis=0, keepdims=True)
        sel = order == pos
        vals.append(m)
        picks.append(pos if payload is None
                     else jnp.max(jnp.where(sel, payload, -1.0), axis=0, keepdims=True))
        cur = jnp.where(sel, -jnp.inf, cur)
    return jnp.concatenate(vals, axis=0), jnp.concatenate(picks, axis=0)


def _pair_slabs(first, second, combine, fill):
    t = first.shape[1]
    r8 = lax.broadcasted_iota(jnp.int32, (SUBLANES, t), 0)
    slabs = [combine(first[0:1], second)]
    for a in range(1, SUBLANES):
        keep = r8 < PEER_TOPK // (a + 1)
        slabs.append(jnp.where(keep, combine(first[a:a + 1], second[0:SUBLANES]), fill))
    slabs.append(combine(first[SUBLANES:], second[0:1]))
    return jnp.concatenate(slabs, axis=0)


def _route_body(x_ref, wq_ref, sk_ref, idx_ref, gate_ref):
    xb = x_ref[...].astype(BF16)
    t = xb.shape[0]
    key_order = lax.broadcasted_iota(jnp.int32, (PEER_NKEYS, t), 0).astype(F32)
    r8 = lax.broadcasted_iota(jnp.int32, (SUBLANES, t), 0)
    flat = [lax.broadcasted_iota(jnp.int32, (PEER_TOPK, t), 0)]
    for a in range(1, SUBLANES):
        flat.append(jnp.where(r8 < PEER_TOPK // (a + 1), r8 + a * PEER_TOPK, PEER_TOPK * PEER_TOPK))
    flat.append((r8 + SUBLANES) * PEER_TOPK)
    flat = jnp.concatenate(flat, axis=0).astype(F32)
    picked, gated = [], []
    for hh in range(PEER_HEADS):
        w = 2 * PEER_HALF
        qb = jnp.dot(xb, wq_ref[:, hh * w:(hh + 1) * w], preferred_element_type=F32).astype(BF16)
        tops = []
        for p in range(2):
            s = lax.dot_general(sk_ref[hh, p], qb[:, p * PEER_HALF:(p + 1) * PEER_HALF], _NT,
                                preferred_element_type=F32)
            tops.append(_top16_rows(s, key_order, None))
        (v0, i0), (v1, i1) = tops
        cand = _pair_slabs(v0, v1, lambda x, y: x + y, -jnp.inf)
        cidx = _pair_slabs(i0 * float(PEER_NKEYS), i1, lambda x, y: x + y, -1.0)
        best, bidx = _top16_rows(cand, flat, cidx)
        e = jnp.exp(best - best[0:1])
        gated.append(e / jnp.sum(e, axis=0, keepdims=True))
        picked.append(bidx)
    idx_ref[...] = jnp.concatenate(picked, axis=0).T.astype(jnp.int32)
    gate_ref[...] = jnp.concatenate(gated, axis=0).T


def _route(x1, w_q, sub_keys, *, tt=256):
    T = x1.shape[0]
    E = PEER_EXPERTS_PER_TOKEN
    return pl.pallas_call(
        _route_body,
        grid=(T // tt,),
        in_specs=[
            pl.BlockSpec((tt, D_MODEL), lambda i: (i, 0)),
            pl.BlockSpec(w_q.shape, lambda i: (0, 0)),
            pl.BlockSpec(sub_keys.shape, lambda i: (0, 0, 0, 0)),
        ],
        out_specs=[
            pl.BlockSpec((tt, E), lambda i: (i, 0)),
            pl.BlockSpec((tt, E), lambda i: (i, 0)),
        ],
        out_shape=[
            jax.ShapeDtypeStruct((T, E), jnp.int32),
            jax.ShapeDtypeStruct((T, E), F32),
        ],
        compiler_params=pltpu.CompilerParams(
            dimension_semantics=("arbitrary",), vmem_limit_bytes=VMEM_LIMIT),
        name="peer_route",
    )(x1, w_q, sub_keys)


def _peer_body(idx_cur, idx_nxt, x_ref, g_ref, lng_ref, lnb_ref, tab_ref, o_ref, *scratch,
               tb, nsteps):
    E = PEER_EXPERTS_PER_TOKEN
    bufs, sem = scratch[:PEER_RING], scratch[PEER_RING]
    i = pl.program_id(0)

    def issue(idx_ref, row, q, t, part):
        for j in range(part * (E // 2), (part + 1) * (E // 2)):
            a, r = divmod(j, SUBLANES)
            pltpu.make_async_copy(tab_ref.at[idx_ref[row, j]],
                                  bufs[q].at[t, a, :, pl.ds(2 * r, 2), :],
                                  sem.at[q]).start(priority=j % 2)

    def wait(q):
        pltpu.make_async_copy(bufs[q], bufs[q], sem.at[q]).wait()

    @pl.when(i == 0)
    def _():
        def prime(t, carry):
            for q in range(PEER_LOOKAHEAD):
                issue(idx_cur, q * tb + t, q, t, 0)
                issue(idx_cur, q * tb + t, q, t, 1)
            return carry
        lax.fori_loop(0, tb, prime, 0)

    rowi = lax.broadcasted_iota(jnp.int32, (tb, 2 * E), 0)
    spread = (lax.broadcasted_iota(jnp.int32, (E, 2 * E), 1)
              == 2 * lax.broadcasted_iota(jnp.int32, (E, 2 * E), 0)).astype(F32)

    def weights(q, t):
        return jnp.concatenate(
            [bufs[q][t, :, c].reshape(2 * E, LANES) for c in range(FEAT_CHUNKS)], axis=1)

    for q in range(PEER_RING):
        r0 = q * tb
        ahead = q + PEER_LOOKAHEAD
        if ahead < PEER_RING:
            prefetch = functools.partial(issue, idx_cur, q=ahead)
        else:
            prefetch = functools.partial(issue, idx_nxt, q=ahead - PEER_RING)
        row0 = (ahead % PEER_RING) * tb

        wait(q)
        xb = x_ref[r0:r0 + tb, :].astype(BF16)
        h = jnp.zeros((tb, 2 * E), F32)
        for t in range(tb):
            prefetch(row=row0 + t, t=t, part=0)
            ht = lax.dot_general(xb, weights(q, t), _NT, preferred_element_type=F32)
            h = jnp.where(rowi == t, ht, h)
        gelu = 0.5 * h * (1.0 + lax.erf(h * math.sqrt(0.5)))
        gates = jnp.dot(g_ref[r0:r0 + tb, :], spread, precision=lax.Precision.HIGHEST,
                        preferred_element_type=F32)
        h = pltpu.roll(gelu * gates, 1, axis=1).astype(BF16)
        y = jnp.zeros((tb, D_MODEL), F32)
        for t in range(tb):
            prefetch(row=row0 + t, t=t, part=1)
            y = y + jnp.dot(jnp.where(rowi == t, h, jnp.zeros_like(h)), weights(q, t),
                            preferred_element_type=F32)
        o_ref[r0:r0 + tb, :] = _layer_norm(DEEPNORM_ALPHA * x_ref[r0:r0 + tb, :] + y,
                                           lng_ref[...], lnb_ref[...])

    @pl.when(i == nsteps - 1)
    def _():
        for q in range(PEER_LOOKAHEAD):
            wait(q)


def _peer(idx, gates, x1, table, ln_g, ln_b, *, tb=16):
    T = x1.shape[0]
    E = PEER_EXPERTS_PER_TOKEN
    step = PEER_RING * tb
    n = T // step
    row = lambda i: (i, 0)
    const = lambda i: (0, 0)
    ring_buf = pltpu.VMEM((tb, E // SUBLANES, FEAT_CHUNKS, 2 * SUBLANES, LANES), BF16)
    return pl.pallas_call(
        functools.partial(_peer_body, tb=tb, nsteps=n),
        grid=(n,),
        in_specs=[
            pl.BlockSpec((step, E), row, memory_space=pltpu.SMEM),
            pl.BlockSpec((step, E), lambda i: (jnp.minimum(i + 1, n - 1), 0),
                         memory_space=pltpu.SMEM),
            pl.BlockSpec((step, D_MODEL), row),
            pl.BlockSpec((step, E), row),
            pl.BlockSpec((1, D_MODEL), const),
            pl.BlockSpec((1, D_MODEL), const),
            pl.BlockSpec(memory_space=pl.ANY),
        ],
        out_specs=pl.BlockSpec((step, D_MODEL), row),
        out_shape=jax.ShapeDtypeStruct((T, D_MODEL), F32),
        scratch_shapes=[ring_buf] * PEER_RING + [pltpu.SemaphoreType.DMA((PEER_RING,))],
        compiler_params=pltpu.CompilerParams(
            dimension_semantics=("arbitrary",), vmem_limit_bytes=VMEM_LIMIT),
        name="peer_experts",
    )(idx, idx, x1, gates, ln_g, ln_b, table)


def kernel(x, ln_in_g, ln_in_b, w_in, fox_fgate_b, hgrn_fgate_b, hgrn_lb_logits, fox_out_g,
           hgrn_out_g, w_out, ln_mix_g, ln_mix_b, peer_w_q, peer_sub_keys, peer_u, peer_v,
           ln_ffn_g, ln_ffn_b):
    B, S, D = x.shape
    T = B * S
    tm = 256
    row = lambda a: a.reshape(1, -1).astype(F32)

    w = w_in[0]
    o1 = 3 * FOX_WIDTH
    o2 = o1 + FOX_HEADS
    w_main = jnp.concatenate([w[:, :o1], w[:, o2:]], axis=1).astype(BF16)
    w_ft = jnp.zeros((GATE_ROWS, D), F32).at[:FOX_HEADS].set(w[:, o1:o2].T).astype(BF16)
    fb = jnp.zeros((GATE_ROWS,), F32).at[:FOX_HEADS].set(fox_fgate_b[0])
    fb = jnp.broadcast_to(fb[:, None], (GATE_ROWS, tm))

    xn, q, k, v, c, hq, hf, hi, hg = _inproj(
        x.reshape(T, D), row(ln_in_g), row(ln_in_b), w_main, w_ft, fb, seq=S, tm=tm)

    c = c[:FOX_HEADS].reshape(FOX_HEADS // 2, 2, B, S).transpose(2, 0, 1, 3)
    fox_o = _fox_attention(q.reshape(B, S, -1), k.reshape(B, S, -1), v.reshape(B, S, -1), c,
                           row(fox_out_g[0]))

    lb = jnp.cumsum(jax.nn.softmax(hgrn_lb_logits.astype(F32), axis=0), axis=0)[0]
    r3 = lambda a: a.reshape(B, S, -1)
    hgrn_o = _hgrn(r3(hq), r3(hf), r3(hi), r3(hg), row(hgrn_fgate_b[0]), row(lb),
                   row(hgrn_out_g[0]))

    wo = w_out[0].astype(BF16)
    x1 = _outproj(fox_o.reshape(T, -1), hgrn_o.reshape(T, -1), xn, wo[:FOX_WIDTH], wo[FOX_WIDTH:],
                  row(ln_mix_g[0]), row(ln_mix_b[0]))

    idx, gates = _route(x1, peer_w_q[0].astype(BF16), peer_sub_keys[0].astype(BF16))
    n_exp = peer_u.shape[1]
    table = jnp.stack([peer_u[0].reshape(n_exp, FEAT_CHUNKS, LANES),
                       peer_v[0].reshape(n_exp, FEAT_CHUNKS, LANES)], axis=2).astype(BF16)

    out = _peer(idx, gates, x1, table, row(ln_ffn_g[0]), row(ln_ffn_b[0]))
    return out.reshape(B, S, D)
```

```python
import functools
import math

import jax
import jax.numpy as jnp
from jax import lax
from jax.experimental import pallas as pl
from jax.experimental.pallas import tpu as pltpu

F32 = jnp.float32
BF16 = jnp.bfloat16

D_MODEL = 1024
FOX_HEADS = 8
FOX_HEAD_DIM = 64
FOX_WIDTH = FOX_HEADS * FOX_HEAD_DIM
HGRN_HEADS = 4
HGRN_DIM = 128
HGRN_WIDTH = HGRN_HEADS * HGRN_DIM
PEER_HEADS = 8
PEER_NKEYS = 128
PEER_TOPK = 16
PEER_HALF = 128
PEER_EXPERTS_PER_TOKEN = PEER_HEADS * PEER_TOPK
LN_EPS = 1e-5
RMS_EPS = 1e-6
DEPTH = 1
DEEPNORM_ALPHA = (2 * DEPTH) ** 0.25
LOG2E = math.log2(math.e)

LANES = 128
SUBLANES = 8
FEAT_CHUNKS = D_MODEL // LANES
GATE_ROWS = 16
PEER_RING = 4
PEER_LOOKAHEAD = 2
HGRN_CHUNK = 128
HGRN_SUB = 16
VMEM_LIMIT = 48 * 1024 * 1024

_NT = (((1,), (1,)), ((), ()))


def _layer_norm(x, g, b):
    mu = jnp.mean(x, axis=-1, keepdims=True)
    xc = x - mu
    var = jnp.mean(xc * xc, axis=-1, keepdims=True)
    return xc * lax.rsqrt(var + LN_EPS) * g + b


def _log_sigmoid_parts(z):
    e = jnp.exp(-jnp.abs(z))
    return jnp.minimum(z, 0.0) - jnp.log(1.0 + e), e


def _inproj_body(x_ref, g_ref, b_ref, wm_ref, wft_ref, fb_ref,
                 xn_ref, q_ref, k_ref, v_ref, c_ref, hq_ref, hf_ref, hi_ref, hg_ref,
                 carry_ref, *, tm, steps_per_batch):
    i = pl.program_id(0)
    xn = _layer_norm(x_ref[...], g_ref[...], b_ref[...])
    xn_ref[...] = xn
    xb = xn.astype(BF16)

    def seg(j):
        return jnp.dot(xb, wm_ref[:, j * 512:(j + 1) * 512], preferred_element_type=F32)

    qs = seg(0) * (FOX_HEAD_DIM ** -0.5 * LOG2E)
    low = lax.broadcasted_iota(jnp.int32, (tm, LANES), 1) < FOX_HEAD_DIM
    for p in range(FOX_HEADS // 2):
        s = qs[:, p * LANES:(p + 1) * LANES]
        q_ref[:, (2 * p) * LANES:(2 * p + 1) * LANES] = jnp.where(low, s, 0.0).astype(BF16)
        q_ref[:, (2 * p + 1) * LANES:(2 * p + 2) * LANES] = jnp.where(low, 0.0, s).astype(BF16)
    k_ref[...] = seg(1).astype(BF16)
    v_ref[...] = seg(2).astype(BF16)
    hq_ref[...] = seg(3)
    hf_ref[...] = seg(4)
    hi_ref[...] = seg(5)
    hg_ref[...] = seg(6)

    z = lax.dot_general(wft_ref[...], xb, _NT, preferred_element_type=F32) + fb_ref[...]
    lf, _ = _log_sigmoid_parts(z)
    r = lax.broadcasted_iota(jnp.int32, (tm, tm), 0)
    c = lax.broadcasted_iota(jnp.int32, (tm, tm), 1)
    upper = (r <= c).astype(F32)
    cs = jnp.dot(lf, upper, precision=lax.Precision.HIGHEST, preferred_element_type=F32)

    @pl.when(i % steps_per_batch == 0)
    def _():
        carry_ref[...] = jnp.zeros_like(carry_ref)

    cs = cs + carry_ref[:, 0:1]
    carry_ref[...] = jnp.broadcast_to(cs[:, tm - 1:tm], carry_ref.shape)
    c_ref[...] = cs * LOG2E


def _inproj(x2, ln_g, ln_b, w_main, w_ft, fgate_b, *, seq, tm=256):
    T = x2.shape[0]
    n = T // tm
    row = lambda i: (i, 0)
    const = lambda i: (0, 0)
    f32_512 = jax.ShapeDtypeStruct((T, 512), F32)
    bf_512 = jax.ShapeDtypeStruct((T, 512), BF16)
    blk512 = pl.BlockSpec((tm, 512), row)
    return pl.pallas_call(
        functools.partial(_inproj_body, tm=tm, steps_per_batch=seq // tm),
        grid=(n,),
        in_specs=[
            pl.BlockSpec((tm, D_MODEL), row),
            pl.BlockSpec((1, D_MODEL), const),
            pl.BlockSpec((1, D_MODEL), const),
            pl.BlockSpec(w_main.shape, const),
            pl.BlockSpec(w_ft.shape, const),
            pl.BlockSpec((GATE_ROWS, tm), const),
        ],
        out_specs=[
            pl.BlockSpec((tm, D_MODEL), row),
            pl.BlockSpec((tm, 2 * FOX_WIDTH), row),
            blk512, blk512,
            pl.BlockSpec((GATE_ROWS, tm), lambda i: (0, i)),
            blk512, blk512, blk512, blk512,
        ],
        out_shape=[
            jax.ShapeDtypeStruct((T, D_MODEL), F32),
            jax.ShapeDtypeStruct((T, 2 * FOX_WIDTH), BF16),
            bf_512, bf_512,
            jax.ShapeDtypeStruct((GATE_ROWS, T), F32),
            f32_512, f32_512, f32_512, f32_512,
        ],
        scratch_shapes=[pltpu.VMEM((GATE_ROWS, LANES), F32)],
        compiler_params=pltpu.CompilerParams(
            dimension_semantics=("arbitrary",), vmem_limit_bytes=VMEM_LIMIT),
        name="ln_inproj",
    )(x2, ln_g, ln_b, w_main, w_ft, fgate_b)


def _fox_body(q_ref, k_ref, v_ref, c_ref, g_ref, o_ref, *, tq):
    qi = pl.program_id(2)
    q0 = pl.multiple_of(qi * tq, tq)
    lane = lax.broadcasted_iota(jnp.int32, (tq, LANES), 1)
    row = lax.broadcasted_iota(jnp.int32, (tq, tq), 0)
    col = lax.broadcasted_iota(jnp.int32, (tq, tq), 1)
    causal = col <= row
    qs = [q_ref[:, hh * LANES:(hh + 1) * LANES] for hh in range(2)]
    c0 = [c_ref[hh:hh + 1, pl.ds(q0, LANES)][:, 0:1] for hh in range(2)]

    def block(j0, carry, masked):
        kj = k_ref[pl.ds(j0, tq), :]
        vj = v_ref[pl.ds(j0, tq), :]
        out = []
        for hh in range(2):
            m, l, acc = carry[hh]
            s = lax.dot_general(qs[hh], kj, _NT, preferred_element_type=F32)
            z = s + (c0[hh] - c_ref[hh:hh + 1, pl.ds(j0, tq)])
            if masked:
                z = jnp.where(causal, z, -jnp.inf)
            m_new = jnp.maximum(m, jnp.max(z, axis=-1, keepdims=True))
            alpha = jnp.exp2(m - m_new)
            p = jnp.exp2(z - m_new)
            l = alpha * l + jnp.sum(p, axis=-1, keepdims=True)
            acc = alpha * acc + jnp.dot(p.astype(BF16), vj, preferred_element_type=F32)
            out.append((m_new, l, acc))
        return tuple(out)

    def step(j, carry):
        return block(pl.multiple_of(j * tq, tq), carry, masked=False)

    init = (jnp.full((tq, 1), -jnp.inf, F32), jnp.zeros((tq, 1), F32),
            jnp.zeros((tq, LANES), F32))
    carry = lax.fori_loop(0, qi, step, (init, init))
    carry = block(q0, carry, masked=True)

    out = jnp.zeros((tq, LANES), F32)
    for hh in range(2):
        _, l, acc = carry[hh]
        mine = (lane < FOX_HEAD_DIM) if hh == 0 else (lane >= FOX_HEAD_DIM)
        o = jnp.where(mine, acc / l, 0.0)
        ms = jnp.sum(o * o, axis=-1, keepdims=True) * (1.0 / FOX_HEAD_DIM)
        out = out + o * lax.rsqrt(ms + RMS_EPS)
    o_ref[...] = (out * g_ref[...]).astype(BF16)


def _fox_attention(q, k, v, c, gain, *, tq=1024):
    B, S, _ = k.shape
    pairs = FOX_HEADS // 2
    return pl.pallas_call(
        functools.partial(_fox_body, tq=tq),
        grid=(B, pairs, S // tq),
        in_specs=[
            pl.BlockSpec((None, tq, 2 * LANES), lambda b, p, i: (b, i, p)),
            pl.BlockSpec((None, S, LANES), lambda b, p, i: (b, 0, p)),
            pl.BlockSpec((None, S, LANES), lambda b, p, i: (b, 0, p)),
            pl.BlockSpec((None, None, 2, S), lambda b, p, i: (b, p, 0, 0)),
            pl.BlockSpec((1, LANES), lambda b, p, i: (0, p)),
        ],
        out_specs=pl.BlockSpec((None, tq, LANES), lambda b, p, i: (b, i, p)),
        out_shape=jax.ShapeDtypeStruct((B, S, FOX_WIDTH), BF16),
        compiler_params=pltpu.CompilerParams(
            dimension_semantics=("arbitrary", "arbitrary", "arbitrary"),
            vmem_limit_bytes=VMEM_LIMIT),
        name="fox_attention",
    )(q, k, v, c, gain)


def _hgrn_body(hq_ref, hf_ref, hi_ref, hg_ref, fb_ref, lb_ref, og_ref, o_ref, state_ref, *, rows):
    C = HGRN_CHUNK
    n_ch = rows // C
    n_sub = C // HGRN_SUB

    @pl.when(pl.program_id(1) == 0)
    def _():
        state_ref[...] = jnp.zeros_like(state_ref)

    r = lax.broadcasted_iota(jnp.int32, (C, C), 0)
    c = lax.broadcasted_iota(jnp.int32, (C, C), 1)
    lower = (c <= r).astype(F32)
    causal = c <= r
    srow = lax.broadcasted_iota(jnp.int32, (C, HGRN_DIM), 0)

    lb = lb_ref[...]
    log_lb = jnp.log(lb)
    log_1mlb = jnp.log(1.0 - lb)
    z = hf_ref[...] + fb_ref[...]
    ls, e = _log_sigmoid_parts(z)
    t = log_1mlb + ls
    g_all = jnp.maximum(log_lb, t) + jnp.log(1.0 + jnp.exp(-jnp.abs(log_lb - t)))
    k_all = (1.0 - lb) * jnp.where(z >= 0.0, e, 1.0) / (1.0 + e)
    b_all = [jnp.dot(lower, g_all[ch * C:(ch + 1) * C], precision=lax.Precision.HIGHEST,
                     preferred_element_type=F32) for ch in range(n_ch)]

    items = [(h, ch) for ch in range(n_ch) for h in range(HGRN_HEADS)]
    q, kk, vb, b, scores = {}, {}, {}, {}, {}
    for it in items:
        h, ch = it
        cols = slice(h * HGRN_DIM, (h + 1) * HGRN_DIM)
        rs = slice(ch * C, (ch + 1) * C)
        q[it] = hq_ref[rs, cols]
        kk[it] = k_all[rs, cols]
        vb[it] = hi_ref[rs, cols].astype(BF16)
        b[it] = b_all[ch][:, cols]
    for it in items:
        parts = []
        for blk in range(n_sub):
            lo = blk * HGRN_SUB
            hi = lo + HGRN_SUB
            ref = b[it][lo - 1:lo, :] if blk > 0 else jnp.zeros((1, HGRN_DIM), F32)
            qs = (q[it][lo:hi] * jnp.exp(b[it][lo:hi] - ref)).astype(BF16)
            ks = (kk[it] * jnp.exp(jnp.where(srow < hi, ref - b[it], -1e30))).astype(BF16)
            parts.append(lax.dot_general(qs, ks, _NT, preferred_element_type=F32))
        scores[it] = jnp.where(causal, jnp.concatenate(parts, axis=0), 0.0).astype(BF16)
    o_intra = {it: jnp.dot(scores[it], vb[it], preferred_element_type=F32) for it in items}
    kv = {}
    for it in items:
        kdec = kk[it] * jnp.exp(b[it][C - 1:C, :] - b[it])
        kv[it] = jnp.dot(kdec.T.astype(BF16), vb[it], preferred_element_type=F32)
    for h in range(HGRN_HEADS):
        cols = slice(h * HGRN_DIM, (h + 1) * HGRN_DIM)
        state = state_ref[h]
        for ch in range(n_ch):
            it = (h, ch)
            rs = slice(ch * C, (ch + 1) * C)
            o = o_intra[it] + jnp.dot((q[it] * jnp.exp(b[it])).astype(BF16), state.astype(BF16),
                                      preferred_element_type=F32)
            decay = jnp.broadcast_to(jnp.exp(b[it][C - 1:C, :]), (HGRN_DIM, HGRN_DIM)).T
            state = decay * state + kv[it]
            ms = jnp.mean(o * o, axis=-1, keepdims=True)
            gate_in = hg_ref[rs, cols]
            silu = gate_in / (1.0 + jnp.exp(-gate_in))
            o_ref[rs, cols] = (o * lax.rsqrt(ms + RMS_EPS) * og_ref[:, cols] * silu).astype(BF16)
        state_ref[h] = state


def _hgrn(hq, hf, hi, hg, fgate_b, lb, out_g, *, rows=256):
    B, S, W = hq.shape
    blk = pl.BlockSpec((None, rows, W), lambda b, i: (b, i, 0))
    vec = pl.BlockSpec((1, W), lambda b, i: (0, 0))
    return pl.pallas_call(
        functools.partial(_hgrn_body, rows=rows),
        grid=(B, S // rows),
        in_specs=[blk, blk, blk, blk, vec, vec, vec],
        out_specs=blk,
        out_shape=jax.ShapeDtypeStruct((B, S, W), BF16),
        scratch_shapes=[pltpu.VMEM((HGRN_HEADS, HGRN_DIM, HGRN_DIM), F32)],
        compiler_params=pltpu.CompilerParams(
            dimension_semantics=("arbitrary", "arbitrary"), vmem_limit_bytes=VMEM_LIMIT),
        name="hgrn2",
    )(hq, hf, hi, hg, fgate_b, lb, out_g)


def _outproj_body(fo_ref, ho_ref, xn_ref, wa_ref, wb_ref, g_ref, b_ref, o_ref):
    mix = jnp.dot(fo_ref[...], wa_ref[...], preferred_element_type=F32)
    mix = mix + jnp.dot(ho_ref[...], wb_ref[...], preferred_element_type=F32)
    o_ref[...] = _layer_norm(DEEPNORM_ALPHA * xn_ref[...] + mix, g_ref[...], b_ref[...])


def _outproj(fox_o, hgrn_o, xn, w_a, w_b, ln_g, ln_b, *, tm=512):
    T = xn.shape[0]
    row = lambda i: (i, 0)
    const = lambda i: (0, 0)
    return pl.pallas_call(
        _outproj_body,
        grid=(T // tm,),
        in_specs=[
            pl.BlockSpec((tm, 512), row), pl.BlockSpec((tm, 512), row),
            pl.BlockSpec((tm, D_MODEL), row),
            pl.BlockSpec((512, D_MODEL), const), pl.BlockSpec((512, D_MODEL), const),
            pl.BlockSpec((1, D_MODEL), const), pl.BlockSpec((1, D_MODEL), const),
        ],
        out_specs=pl.BlockSpec((tm, D_MODEL), row),
        out_shape=jax.ShapeDtypeStruct((T, D_MODEL), F32),
        compiler_params=pltpu.CompilerParams(
            dimension_semantics=("arbitrary",), vmem_limit_bytes=VMEM_LIMIT),
        name="outproj_ln",
    )(fox_o, hgrn_o, xn, w_a, w_b, ln_g, ln_b)


def _top16_rows(scores, order, payload):
    big = float(PEER_TOPK * PEER_NKEYS)
    vals, picks = [], []
    cur = scores
    for _ in range(PEER_TOPK):
        m = jnp.max(cur, axis=0, keepdims=True)
        pos = jnp.min(jnp.where(cur == m, order, big), axis=0, keepdims=True)
        sel = order == pos
        vals.append(m)
        picks.append(pos if payload is None
                     else jnp.max(jnp.where(sel, payload, -1.0), axis=0, keepdims=True))
        cur = jnp.where(sel, -jnp.inf, cur)
    return jnp.concatenate(vals, axis=0), jnp.concatenate(picks, axis=0)


def _pair_slabs(first, second, combine, fill):
    t = first.shape[1]
    r8 = lax.broadcasted_iota(jnp.int32, (SUBLANES, t), 0)
    slabs = [combine(first[0:1], second)]
    for a in range(1, SUBLANES):
        keep = r8 < PEER_TOPK // (a + 1)
        slabs.append(jnp.where(keep, combine(first[a:a + 1], second[0:SUBLANES]), fill))
    slabs.append(combine(first[SUBLANES:], second[0:1]))
    return jnp.concatenate(slabs, axis=0)


def _route_body(x_ref, wq_ref, sk_ref, idx_ref, gate_ref):
    xb = x_ref[...].astype(BF16)
    t = xb.shape[0]
    key_order = lax.broadcasted_iota(jnp.int32, (PEER_NKEYS, t), 0).astype(F32)
    r8 = lax.broadcasted_iota(jnp.int32, (SUBLANES, t), 0)
    flat = [lax.broadcasted_iota(jnp.int32, (PEER_TOPK, t), 0)]
    for a in range(1, SUBLANES):
        flat.append(jnp.where(r8 < PEER_TOPK // (a + 1), r8 + a * PEER_TOPK, PEER_TOPK * PEER_TOPK))
    flat.append((r8 + SUBLANES) * PEER_TOPK)
    flat = jnp.concatenate(flat, axis=0).astype(F32)
    picked, gated = [], []
    for hh in range(PEER_HEADS):
        w = 2 * PEER_HALF
        qb = jnp.dot(xb, wq_ref[:, hh * w:(hh + 1) * w], preferred_element_type=F32).astype(BF16)
        tops = []
        for p in range(2):
            s = lax.dot_general(sk_ref[hh, p], qb[:, p * PEER_HALF:(p + 1) * PEER_HALF], _NT,
                                preferred_element_type=F32)
            tops.append(_top16_rows(s, key_order, None))
        (v0, i0), (v1, i1) = tops
        cand = _pair_slabs(v0, v1, lambda x, y: x + y, -jnp.inf)
        cidx = _pair_slabs(i0 * float(PEER_NKEYS), i1, lambda x, y: x + y, -1.0)
        best, bidx = _top16_rows(cand, flat, cidx)
        e = jnp.exp(best - best[0:1])
        gated.append(e / jnp.sum(e, axis=0, keepdims=True))
        picked.append(bidx)
    idx_ref[...] = jnp.concatenate(picked, axis=0).T.astype(jnp.int32)
    gate_ref[...] = jnp.concatenate(gated, axis=0).T


def _route(x1, w_q, sub_keys, *, tt=256):
    T = x1.shape[0]
    E = PEER_EXPERTS_PER_TOKEN
    return pl.pallas_call(
        _route_body,
        grid=(T // tt,),
        in_specs=[
            pl.BlockSpec((tt, D_MODEL), lambda i: (i, 0)),
            pl.BlockSpec(w_q.shape, lambda i: (0, 0)),
            pl.BlockSpec(sub_keys.shape, lambda i: (0, 0, 0, 0)),
        ],
        out_specs=[
            pl.BlockSpec((tt, E), lambda i: (i, 0)),
            pl.BlockSpec((tt, E), lambda i: (i, 0)),
        ],
        out_shape=[
            jax.ShapeDtypeStruct((T, E), jnp.int32),
            jax.ShapeDtypeStruct((T, E), F32),
        ],
        compiler_params=pltpu.CompilerParams(
            dimension_semantics=("arbitrary",), vmem_limit_bytes=VMEM_LIMIT),
        name="peer_route",
    )(x1, w_q, sub_keys)


def _peer_body(idx_cur, idx_nxt, x_ref, g_ref, lng_ref, lnb_ref, tab_ref, o_ref, *scratch,
               tb, nsteps):
    E = PEER_EXPERTS_PER_TOKEN
    bufs, sem = scratch[:PEER_RING], scratch[PEER_RING]
    i = pl.program_id(0)

    def issue(idx_ref, row, q, t, part):
        for j in range(part * (E // 2), (part + 1) * (E // 2)):
            a, r = divmod(j, SUBLANES)
            pltpu.make_async_copy(tab_ref.at[idx_ref[row, j]],
                                  bufs[q].at[t, a, :, pl.ds(2 * r, 2), :],
                                  sem.at[q]).start(priority=j % 2)

    def wait(q):
        pltpu.make_async_copy(bufs[q], bufs[q], sem.at[q]).wait()

    @pl.when(i == 0)
    def _():
        def prime(t, carry):
            for q in range(PEER_LOOKAHEAD):
                issue(idx_cur, q * tb + t, q, t, 0)
                issue(idx_cur, q * tb + t, q, t, 1)
            return carry
        lax.fori_loop(0, tb, prime, 0)

    rowi = lax.broadcasted_iota(jnp.int32, (tb, 2 * E), 0)
    spread = (lax.broadcasted_iota(jnp.int32, (E, 2 * E), 1)
              == 2 * lax.broadcasted_iota(jnp.int32, (E, 2 * E), 0)).astype(F32)

    def weights(q, t):
        return jnp.concatenate(
            [bufs[q][t, :, c].reshape(2 * E, LANES) for c in range(FEAT_CHUNKS)], axis=1)

    for q in range(PEER_RING):
        r0 = q * tb
        ahead = q + PEER_LOOKAHEAD
        if ahead < PEER_RING:
            prefetch = functools.partial(issue, idx_cur, q=ahead)
        else:
            prefetch = functools.partial(issue, idx_nxt, q=ahead - PEER_RING)
        row0 = (ahead % PEER_RING) * tb

        wait(q)
        xb = x_ref[r0:r0 + tb, :].astype(BF16)
        h = jnp.zeros((tb, 2 * E), F32)
        for t in range(tb):
            prefetch(row=row0 + t, t=t, part=0)
            ht = lax.dot_general(xb, weights(q, t), _NT, preferred_element_type=F32)
            h = jnp.where(rowi == t, ht, h)
        gelu = 0.5 * h * (1.0 + lax.erf(h * math.sqrt(0.5)))
        gates = jnp.dot(g_ref[r0:r0 + tb, :], spread, precision=lax.Precision.HIGHEST,
                        preferred_element_type=F32)
        h = pltpu.roll(gelu * gates, 1, axis=1).astype(BF16)
        y = jnp.zeros((tb, D_MODEL), F32)
        for t in range(tb):
            prefetch(row=row0 + t, t=t, part=1)
            y = y + jnp.dot(jnp.where(rowi == t, h, jnp.zeros_like(h)), weights(q, t),
                            preferred_element_type=F32)
        o_ref[r0:r0 + tb, :] = _layer_norm(DEEPNORM_ALPHA * x_ref[r0:r0 + tb, :] + y,
                                           lng_ref[...], lnb_ref[...])

    @pl.when(i == nsteps - 1)
    def _():
        for q in range(PEER_LOOKAHEAD):
            wait(q)


def _peer(idx, gates, x1, table, ln_g, ln_b, *, tb=16):
    T = x1.shape[0]
    E = PEER_EXPERTS_PER_TOKEN
    step = PEER_RING * tb
    n = T // step
    row = lambda i: (i, 0)
    const = lambda i: (0, 0)
    ring_buf = pltpu.VMEM((tb, E // SUBLANES, FEAT_CHUNKS, 2 * SUBLANES, LANES), BF16)
    return pl.pallas_call(
        functools.partial(_peer_body, tb=tb, nsteps=n),
        grid=(n,),
        in_specs=[
            pl.BlockSpec((step, E), row, memory_space=pltpu.SMEM),
            pl.BlockSpec((step, E), lambda i: (jnp.minimum(i + 1, n - 1), 0),
                         memory_space=pltpu.SMEM),
            pl.BlockSpec((step, D_MODEL), row),
            pl.BlockSpec((step, E), row),
            pl.BlockSpec((1, D_MODEL), const),
            pl.BlockSpec((1, D_MODEL), const),
            pl.BlockSpec(memory_space=pl.ANY),
        ],
        out_specs=pl.BlockSpec((step, D_MODEL), row),
        out_shape=jax.ShapeDtypeStruct((T, D_MODEL), F32),
        scratch_shapes=[ring_buf] * PEER_RING + [pltpu.SemaphoreType.DMA((PEER_RING,))],
        compiler_params=pltpu.CompilerParams(
            dimension_semantics=("arbitrary",), vmem_limit_bytes=VMEM_LIMIT),
        name="peer_experts",
    )(idx, idx, x1, gates, ln_g, ln_b, table)


def kernel(x, ln_in_g, ln_in_b, w_in, fox_fgate_b, hgrn_fgate_b, hgrn_lb_logits, fox_out_g,
           hgrn_out_g, w_out, ln_mix_g, ln_mix_b, peer_w_q, peer_sub_keys, peer_u, peer_v,
           ln_ffn_g, ln_ffn_b):
    B, S, D = x.shape
    T = B * S
    tm = 256
    row = lambda a: a.reshape(1, -1).astype(F32)

    w = w_in[0]
    o1 = 3 * FOX_WIDTH
    o2 = o1 + FOX_HEADS
    w_main = jnp.concatenate([w[:, :o1], w[:, o2:]], axis=1).astype(BF16)
    w_ft = jnp.zeros((GATE_ROWS, D), F32).at[:FOX_HEADS].set(w[:, o1:o2].T).astype(BF16)
    fb = jnp.zeros((GATE_ROWS,), F32).at[:FOX_HEADS].set(fox_fgate_b[0])
    fb = jnp.broadcast_to(fb[:, None], (GATE_ROWS, tm))

    xn, q, k, v, c, hq, hf, hi, hg = _inproj(
        x.reshape(T, D), row(ln_in_g), row(ln_in_b), w_main, w_ft, fb, seq=S, tm=tm)

    c = c[:FOX_HEADS].reshape(FOX_HEADS // 2, 2, B, S).transpose(2, 0, 1, 3)
    fox_o = _fox_attention(q.reshape(B, S, -1), k.reshape(B, S, -1), v.reshape(B, S, -1), c,
                           row(fox_out_g[0]))

    lb = jnp.cumsum(jax.nn.softmax(hgrn_lb_logits.astype(F32), axis=0), axis=0)[0]
    r3 = lambda a: a.reshape(B, S, -1)
    hgrn_o = _hgrn(r3(hq), r3(hf), r3(hi), r3(hg), row(hgrn_fgate_b[0]), row(lb),
                   row(hgrn_out_g[0]))

    wo = w_out[0].astype(BF16)
    x1 = _outproj(fox_o.reshape(T, -1), hgrn_o.reshape(T, -1), xn, wo[:FOX_WIDTH], wo[FOX_WIDTH:],
                  row(ln_mix_g[0]), row(ln_mix_b[0]))

    idx, gates = _route(x1, peer_w_q[0].astype(BF16), peer_sub_keys[0].astype(BF16))
    n_exp = peer_u.shape[1]
    table = jnp.stack([peer_u[0].reshape(n_exp, FEAT_CHUNKS, LANES),
                       peer_v[0].reshape(n_exp, FEAT_CHUNKS, LANES)], axis=2).astype(BF16)

    out = _peer(idx, gates, x1, table, row(ln_ffn_g[0]), row(ln_ffn_b[0]))
    return out.reshape(B, S, D)
```
